```python
import math
import functools
import numpy as np
import jax
import jax.numpy as jnp
from jax import lax

D_MODEL = 1024
BATCH = 32
SEQ = 2048
DEPTH = 1
DEC_BATCH = 128
DEC_SEQ = 8
PAST_LEN = 8192
PAGE_SIZE = 128

HEAD_DIM = 64
A_WIDTH = D_MODEL // 2
B_WIDTH = D_MODEL - A_WIDTH
A_HEADS = A_WIDTH // (2 * HEAD_DIM)
A_VDIM = 2 * HEAD_DIM
B_HEADS = B_WIDTH // HEAD_DIM
D_FF = -(-(8 * D_MODEL) // (3 * 256)) * 256
IN_SIZES = (A_HEADS * 2 * HEAD_DIM, A_HEADS * 2 * HEAD_DIM, A_HEADS * A_VDIM,
            B_HEADS * HEAD_DIM, B_HEADS * HEAD_DIM, B_HEADS * HEAD_DIM, B_HEADS)
IN_COLS = sum(IN_SIZES)
IN_SPLITS = tuple(int(i) for i in np.cumsum(IN_SIZES)[:-1])
ROPE_THETA = 10000.0
Q_BLOCK = 128
RMS_EPS = 1e-6
NEG_INF = -1e30
ATTN_SCALE = HEAD_DIM ** -0.5

kernel_name = 'hybrid_diff_fox_parallel_heads_step'


def _lambda_init(layer):
    return 0.8 - 0.6 * math.exp(-0.3 * layer)


def _rmsnorm(x, g):
    xf = x.astype(jnp.float32)
    y = xf * lax.rsqrt(jnp.mean(xf * xf, axis=-1, keepdims=True) + RMS_EPS)
    return (y * g.astype(jnp.float32)).astype(x.dtype)


def _rope(x, pos):
    half = x.shape[-1] // 2
    inv = ROPE_THETA ** (-jnp.arange(half, dtype=jnp.float32) / half)
    ang = pos[:, None] * inv[None, :]
    cos = jnp.cos(ang)[None, :, None, :]
    sin = jnp.sin(ang)[None, :, None, :]
    xf = x.astype(jnp.float32)
    x1, x2 = xf[..., :half], xf[..., half:]
    return jnp.concatenate([x1 * cos - x2 * sin, x2 * cos + x1 * sin], axis=-1).astype(x.dtype)


def _project(h, pos, w_in, b_f):
    B, T, _ = h.shape
    proj = jnp.einsum('btd,dc->btc', h, w_in)
    qa, ka, va, qb, kb, vb, fl = jnp.split(proj, IN_SPLITS, axis=-1)
    qa = _rope(qa.reshape(B, T, 2 * A_HEADS, HEAD_DIM), pos).reshape(B, T, A_HEADS, 2 * HEAD_DIM)
    ka = _rope(ka.reshape(B, T, 2 * A_HEADS, HEAD_DIM), pos).reshape(B, T, A_HEADS, 2 * HEAD_DIM)
    va = va.reshape(B, T, A_HEADS, A_VDIM)
    qb = qb.reshape(B, T, B_HEADS, HEAD_DIM)
    kb = kb.reshape(B, T, B_HEADS, HEAD_DIM)
    vb = vb.reshape(B, T, B_HEADS, HEAD_DIM)
    logf = jax.nn.log_sigmoid((fl + b_f).astype(jnp.float32))
    return qa, ka, va, qb, kb, vb, logf


def _scores(q, k):
    return jnp.einsum('bqhd,bkhd->bhqk', q, k).astype(jnp.float32) * ATTN_SCALE


def _prompt_attention(qa, ka, va, qb, kb, vb, logf, lam):
    B, T = qa.shape[:2]
    nb = T // Q_BLOCK
    cum = jnp.cumsum(logf, axis=1)
    ck = cum.transpose(0, 2, 1)[:, :, None, :]
    ka1, ka2 = ka[..., :HEAD_DIM], ka[..., HEAD_DIM:]
    kpos = jnp.arange(T)

    def to_blocks(a):
        return jnp.moveaxis(a.reshape((B, nb, Q_BLOCK) + a.shape[2:]), 1, 0)

    def one_block(args):
        blk, qa_b, qb_b, cq_b = args
        qpos = blk * Q_BLOCK + jnp.arange(Q_BLOCK)
        mask = (kpos[None, :] <= qpos[:, None])[None, None]
        p1 = jax.nn.softmax(jnp.where(mask, _scores(qa_b[..., :HEAD_DIM], ka1), NEG_INF), axis=-1)
        p2 = jax.nn.softmax(jnp.where(mask, _scores(qa_b[..., HEAD_DIM:], ka2), NEG_INF), axis=-1)
        oa = jnp.einsum('bhqk,bkhe->bqhe', (p1 - lam * p2).astype(va.dtype), va)
        sb = _scores(qb_b, kb) + (cq_b.transpose(0, 2, 1)[..., None] - ck)
        pb = jax.nn.softmax(jnp.where(mask, sb, NEG_INF), axis=-1)
        ob = jnp.einsum('bhqk,bkhe->bqhe', pb.astype(vb.dtype), vb)
        return oa, ob

    oa, ob = lax.map(one_block, (jnp.arange(nb), to_blocks(qa), to_blocks(qb), to_blocks(cum)))

    def from_blocks(a):
        return jnp.moveaxis(a, 0, 1).reshape((B, T) + a.shape[3:])

    return from_blocks(oa), from_blocks(ob)


def _online(state, s, v):
    m, l, acc = state
    m_new = jnp.maximum(m, jnp.max(s, axis=-1))
    corr = jnp.exp(m - m_new)
    p = jnp.exp(s - m_new[..., None])
    acc_new = acc * corr[..., None] + jnp.einsum('bhqk,bkhe->bhqe', p, v.astype(jnp.float32))
    return (m_new, l * corr + jnp.sum(p, axis=-1), acc_new)


def _sample_attention(qa, ka, va, qb, kb, vb, logf, lam, *, layer, cache_a_k, cache_a_v,
                      cache_b_k, cache_b_v, cache_b_logf, page_table):
    Bd, Tn = qa.shape[:2]
    n_pages = page_table.shape[1]
    page_size = cache_a_k.shape[2]
    past_logf = cache_b_logf[layer, page_table].astype(jnp.float32)
    past_cum = jnp.cumsum(past_logf.reshape(Bd, n_pages * page_size, B_HEADS), axis=1)
    new_cum = past_cum[:, -1:, :] + jnp.cumsum(logf, axis=1)
    cq = new_cum.transpose(0, 2, 1)[..., None]
    qa1, qa2 = qa[..., :HEAD_DIM], qa[..., HEAD_DIM:]

    def init(n_heads, e):
        return (jnp.full((Bd, n_heads, Tn), NEG_INF, jnp.float32),
                jnp.zeros((Bd, n_heads, Tn), jnp.float32),
                jnp.zeros((Bd, n_heads, Tn, e), jnp.float32))

    def attend(states, ka_k, va_k, kb_k, vb_k, ck_k, mask):
        sa1, sa2, sb = states
        s1 = _scores(qa1, ka_k[..., :HEAD_DIM])
        s2 = _scores(qa2, ka_k[..., HEAD_DIM:])
        s3 = _scores(qb, kb_k) + (cq - ck_k.transpose(0, 2, 1)[:, :, None, :])
        if mask is not None:
            s1 = jnp.where(mask, s1, NEG_INF)
            s2 = jnp.where(mask, s2, NEG_INF)
            s3 = jnp.where(mask, s3, NEG_INF)
        return (_online(sa1, s1, va_k), _online(sa2, s2, va_k), _online(sb, s3, vb_k))

    def page_step(states, xs):
        pt, ck_p = xs
        states = attend(states, cache_a_k[layer, pt], cache_a_v[layer, pt],
                        cache_b_k[layer, pt], cache_b_v[layer, pt], ck_p, None)
        return states, None

    xs = (page_table.T, jnp.moveaxis(past_cum.reshape(Bd, n_pages, page_size, B_HEADS), 1, 0))
    states, _ = lax.scan(page_step, (init(A_HEADS, A_VDIM), init(A_HEADS, A_VDIM), init(B_HEADS, HEAD_DIM)), xs)
    causal = jnp.tril(jnp.ones((Tn, Tn), dtype=bool))[None, None]
    (_, l1, a1), (_, l2, a2), (_, lb, ab) = attend(states, ka, va, kb, vb, new_cum, causal)
    oa = a1 / l1[..., None] - lam * (a2 / l2[..., None])
    ob = ab / lb[..., None]
    return oa.transpose(0, 2, 1, 3), ob.transpose(0, 2, 1, 3)


def _layer(x, pos, attn_fn, lam, lam_init, g_pre_mix, w_in, b_f, g_subln, w_out,
           g_post_mix, g_pre_ffn, w_gate, w_up, w_down, g_post_ffn):
    B, T, _ = x.shape
    h = _rmsnorm(x, g_pre_mix)
    qa, ka, va, qb, kb, vb, logf = _project(h, pos, w_in, b_f)
    oa, ob = attn_fn(qa, ka, va, qb, kb, vb, logf, lam)
    oa = _rmsnorm(oa, g_subln) * (1.0 - lam_init)
    mix = jnp.concatenate([oa.reshape(B, T, A_WIDTH), ob.reshape(B, T, B_WIDTH)], axis=-1).astype(x.dtype)
    x = x + _rmsnorm(jnp.einsum('btc,cd->btd', mix, w_out), g_post_mix)
    h = _rmsnorm(x, g_pre_ffn)
    f = jax.nn.silu(jnp.einsum('btd,df->btf', h, w_gate)) * jnp.einsum('btd,df->btf', h, w_up)
    x = x + _rmsnorm(jnp.einsum('btf,fd->btd', f, w_down), g_post_ffn)
    return x, (ka, va, kb, vb, logf)


def setup_inputs(seed: int = 0) -> dict:
    key = jax.random.key(seed)
    ks = jax.random.split(key, 24)
    f32 = jnp.float32
    n_pages = PAST_LEN // PAGE_SIZE
    n_used = DEC_BATCH * n_pages
    n_pool = (5 * n_used) // 4

    def nrm(k, shape, scale=1.0):
        return jax.random.normal(k, shape, f32) * scale

    def gain(k, n):
        return 1.0 + 0.05 * nrm(k, (DEPTH, n))

    page_table = jax.random.permutation(ks[7], n_pool)[:n_used].reshape(DEC_BATCH, n_pages).astype(jnp.int32)
    return {
        'x_prompt': nrm(ks[0], (BATCH, SEQ, D_MODEL)),
        'x_sample': nrm(ks[1], (DEC_BATCH, DEC_SEQ, D_MODEL)),
        'cache_a_k': nrm(ks[2], (DEPTH, n_pool, PAGE_SIZE, A_HEADS, 2 * HEAD_DIM)),
        'cache_a_v': nrm(ks[3], (DEPTH, n_pool, PAGE_SIZE, A_HEADS, A_VDIM)),
        'cache_b_k': nrm(ks[4], (DEPTH, n_pool, PAGE_SIZE, B_HEADS, HEAD_DIM)),
        'cache_b_v': nrm(ks[5], (DEPTH, n_pool, PAGE_SIZE, B_HEADS, HEAD_DIM)),
        'cache_b_logf': jax.nn.log_sigmoid(2.5 + nrm(ks[6], (DEPTH, n_pool, PAGE_SIZE, B_HEADS))),
        'page_table': page_table,
        'g_pre_mix': gain(ks[8], D_MODEL),
        'w_in': nrm(ks[9], (DEPTH, D_MODEL, IN_COLS), D_MODEL ** -0.5),
        'b_f': jax.random.uniform(ks[10], (DEPTH, B_HEADS), f32, 1.0, 4.0),
        'lam_q1': nrm(ks[11], (DEPTH, HEAD_DIM), 0.1),
        'lam_k1': nrm(ks[12], (DEPTH, HEAD_DIM), 0.1),
        'lam_q2': nrm(ks[13], (DEPTH, HEAD_DIM), 0.1),
        'lam_k2': nrm(ks[14], (DEPTH, HEAD_DIM), 0.1),
        'g_subln': gain(ks[15], A_VDIM),
        'w_out': nrm(ks[16], (DEPTH, D_MODEL, D_MODEL), D_MODEL ** -0.5),
        'g_post_mix': gain(ks[17], D_MODEL),
        'g_pre_ffn': gain(ks[18], D_MODEL),
        'w_gate': nrm(ks[19], (DEPTH, D_MODEL, D_FF), D_MODEL ** -0.5),
        'w_up': nrm(ks[20], (DEPTH, D_MODEL, D_FF), D_MODEL ** -0.5),
        'w_down': nrm(ks[21], (DEPTH, D_FF, D_MODEL), D_FF ** -0.5),
        'g_post_ffn': gain(ks[22], D_MODEL),
    }


def _stack(rows, i):
    return jnp.stack([r[i] for r in rows], axis=0)


def reference(x_prompt, x_sample, cache_a_k, cache_a_v, cache_b_k, cache_b_v, cache_b_logf, page_table,
              g_pre_mix, w_in, b_f, lam_q1, lam_k1, lam_q2, lam_k2, g_subln, w_out, g_post_mix,
              g_pre_ffn, w_gate, w_up, w_down, g_post_ffn):
    f32 = jnp.float32
    past_len = page_table.shape[1] * cache_a_k.shape[2]
    pos_prompt = jnp.arange(x_prompt.shape[1], dtype=f32)
    pos_sample = past_len + jnp.arange(x_sample.shape[1], dtype=f32)
    yp, ys = x_prompt, x_sample
    rows_p, rows_s = [], []
    for layer in range(DEPTH):
        lam_init = _lambda_init(layer)
        lam = (jnp.exp(jnp.sum(lam_q1[layer].astype(f32) * lam_k1[layer].astype(f32)))
               - jnp.exp(jnp.sum(lam_q2[layer].astype(f32) * lam_k2[layer].astype(f32))) + lam_init)
        weights = (g_pre_mix[layer], w_in[layer], b_f[layer], g_subln[layer], w_out[layer], g_post_mix[layer],
                   g_pre_ffn[layer], w_gate[layer], w_up[layer], w_down[layer], g_post_ffn[layer])
        sample_attn = functools.partial(_sample_attention, layer=layer, cache_a_k=cache_a_k, cache_a_v=cache_a_v,
                                        cache_b_k=cache_b_k, cache_b_v=cache_b_v, cache_b_logf=cache_b_logf,
                                        page_table=page_table)
        yp, rp = _layer(yp, pos_prompt, _prompt_attention, lam, lam_init, *weights)
        ys, rs = _layer(ys, pos_sample, sample_attn, lam, lam_init, *weights)
        rows_p.append(rp)
        rows_s.append(rs)
    return (yp, ys,
            _stack(rows_p, 0), _stack(rows_p, 1), _stack(rows_p, 2), _stack(rows_p, 3), _stack(rows_p, 4),
            _stack(rows_s, 0), _stack(rows_s, 1), _stack(rows_s, 2), _stack(rows_s, 3), _stack(rows_s, 4))
```

```python
import functools
import math

import jax
import jax.numpy as jnp
from jax import lax
from jax.experimental import pallas as pl
from jax.experimental.pallas import tpu as pltpu

F32 = jnp.float32
BF16 = jnp.bfloat16

HEAD_DIM = 64
A_VDIM = 2 * HEAD_DIM
ROPE_THETA = 10000.0
RMS_EPS = 1e-6
NEG_INF = -1e30
ATTN_SCALE = HEAD_DIM ** -0.5

LANES = 128
SUBLANES = 8
GROUP_W = 512
VMEM_LIMIT = 56 * 1024 * 1024


def _lambda_init(layer):
    return 0.8 - 0.6 * math.exp(-0.3 * layer)


def _rms(x, g):
    return x * lax.rsqrt(jnp.mean(x * x, axis=-1, keepdims=True) + RMS_EPS) * g


def _lam_value(lamv_ref, lam_init):
    v = lamv_ref[...]
    a = jnp.sum(v[0:1] * v[1:2], axis=1, keepdims=True)
    b = jnp.sum(v[2:3] * v[3:4], axis=1, keepdims=True)
    return jnp.exp(a) - jnp.exp(b) + lam_init


def _cumsum_lanes(x):
    lane = lax.broadcasted_iota(jnp.int32, x.shape, 1)
    for s in (1, 2, 4, 8, 16, 32, 64):
        x = x + jnp.where(lane >= s, pltpu.roll(x, s, 1), 0.0)
    return x


def _proj_kernel(x_ref, g_ref, w_ref, bf_ref, cos_ref, sin_ref,
                 qa_ref, ka_ref, va_ref, qb_ref, kb_ref, vb_ref, lft_ref, *, transpose_b):
    x = x_ref[...]
    tm = x.shape[0]
    hb = _rms(x, g_ref[...]).astype(BF16)
    cos4 = jnp.concatenate([cos_ref[...]] * 4, axis=1)
    sin4 = jnp.concatenate([sin_ref[...]] * 4, axis=1)
    lane = lax.broadcasted_iota(jnp.int32, (tm, GROUP_W), 1)
    first_half = (lane % HEAD_DIM) < (HEAD_DIM // 2)

    def col(j, width=GROUP_W):
        return jnp.dot(hb, w_ref[:, j * GROUP_W:j * GROUP_W + width], preferred_element_type=F32)

    def rope(y):
        partner = jnp.where(first_half, pltpu.roll(y, GROUP_W - HEAD_DIM // 2, 1),
                            pltpu.roll(y, HEAD_DIM // 2, 1))
        return y * cos4 + partner * sin4

    qa_ref[...] = (rope(col(0)) * ATTN_SCALE).astype(BF16)
    ka_ref[...] = rope(col(1))
    va_ref[...] = col(2)
    qb_ref[...] = (col(3) * ATTN_SCALE).astype(BF16)
    kb = col(4)
    vb = col(5)
    if transpose_b:
        kb_ref[...] = kb.T
        vb_ref[...] = vb.T
    else:
        kb_ref[...] = kb
        vb_ref[...] = vb
    fl = col(6, LANES) + bf_ref[...]
    lf = jnp.minimum(fl, 0.0) - jnp.log1p(jnp.exp(-jnp.abs(fl)))
    lft_ref[...] = lf.T[:SUBLANES, :]


def _proj_call(x3, g, w_bf, bfp, cos_t, sin_t, *, tm, transpose_b):
    B, T, D = x3.shape
    nt = T // tm
    ntab = cos_t.shape[0] // tm
    row = lambda b, i: (b, i, 0)
    colm = lambda b, i: (b, 0, i)
    const = lambda b, i: (0, 0)
    tab = lambda b, i: (i % ntab, 0)
    b_shape = (B, GROUP_W, T) if transpose_b else (B, T, GROUP_W)
    b_spec = (pl.BlockSpec((None, GROUP_W, tm), colm) if transpose_b
              else pl.BlockSpec((None, tm, GROUP_W), row))
    tok = lambda dt: jax.ShapeDtypeStruct((B, T, GROUP_W), dt)
    return pl.pallas_call(
        functools.partial(_proj_kernel, transpose_b=transpose_b),
        grid=(B, nt),
        in_specs=[
            pl.BlockSpec((None, tm, D), row),
            pl.BlockSpec((1, D), const),
            pl.BlockSpec(w_bf.shape, const),
            pl.BlockSpec((1, LANES), const),
            pl.BlockSpec((tm, LANES), tab),
            pl.BlockSpec((tm, LANES), tab),
        ],
        out_specs=[
            pl.BlockSpec((None, tm, GROUP_W), row),
            pl.BlockSpec((None, tm, GROUP_W), row),
            pl.BlockSpec((None, tm, GROUP_W), row),
            pl.BlockSpec((None, tm, GROUP_W), row),
            b_spec,
            b_spec,
            pl.BlockSpec((None, SUBLANES, tm), colm),
        ],
        out_shape=[tok(BF16), tok(F32), tok(F32), tok(BF16),
                   jax.ShapeDtypeStruct(b_shape, F32), jax.ShapeDtypeStruct(b_shape, F32),
                   jax.ShapeDtypeStruct((B, SUBLANES, T), F32)],
        compiler_params=pltpu.CompilerParams(
            dimension_semantics=("arbitrary", "arbitrary"), vmem_limit_bytes=VMEM_LIMIT),
        name="proj",
    )(x3, g, w_bf, bfp, cos_t, sin_t)


_NT = (((1,), (1,)), ((), ()))


def _flash_pair(q_ref, k_sc, v_sc, bias_row, write_out, m_sc, l_sc, acc_sc, *, seq, tq, tk, k_transposed):
    nq = seq // tq
    per_q = tq // tk
    lane = lax.broadcasted_iota(jnp.int32, (tq, LANES), 1)

    def qblock(i, carry):
        q0 = pl.multiple_of(i * tq, tq)
        q = q_ref[pl.ds(q0, tq), :]
        zero = jnp.zeros_like(q)
        qs = (jnp.where(lane < HEAD_DIM, q, zero), jnp.where(lane >= HEAD_DIM, q, zero))
        m_sc[...] = jnp.full(m_sc.shape, NEG_INF, F32)
        l_sc[...] = jnp.zeros(l_sc.shape, F32)
        acc_sc[...] = jnp.zeros(acc_sc.shape, F32)

        def kv(j, masked):
            k0 = pl.multiple_of(j * tk, tk)
            if k_transposed:
                k = k_sc[:, pl.ds(k0, tk)]
                v = v_sc[:, pl.ds(k0, tk)]
            else:
                k = k_sc[pl.ds(k0, tk), :]
                v = v_sc[pl.ds(k0, tk), :]
            if masked:
                rowg = q0 + lax.broadcasted_iota(jnp.int32, (tq, tk), 0)
                colg = k0 + lax.broadcasted_iota(jnp.int32, (tq, tk), 1)
                keep = colg <= rowg
            for c in range(2):
                if k_transposed:
                    s = jnp.dot(qs[c], k, preferred_element_type=F32)
                else:
                    s = lax.dot_general(qs[c], k, _NT, preferred_element_type=F32)
                if bias_row is not None:
                    s = s - bias_row(c, k0)
                if masked:
                    s = jnp.where(keep, s, NEG_INF)
                m_prev = m_sc[c]
                m_new = jnp.maximum(m_prev, jnp.max(s, axis=1, keepdims=True))
                corr = jnp.exp(m_prev - m_new)
                p = jnp.exp(s - m_new)
                l_sc[c] = corr * l_sc[c] + jnp.sum(p, axis=1, keepdims=True)
                pb = p.astype(BF16)
                if k_transposed:
                    pv = lax.dot_general(pb, v, _NT, preferred_element_type=F32)
                else:
                    pv = jnp.dot(pb, v, preferred_element_type=F32)
                acc_sc[c] = acc_sc[c] * corr + pv
                m_sc[c] = m_new

        def full_block(j, c):
            kv(j, False)
            return c

        lax.fori_loop(0, i * per_q, full_block, 0)
        for d in range(per_q):
            kv(i * per_q + d, True)
        write_out(q0, acc_sc[0] / l_sc[0], acc_sc[1] / l_sc[1])
        return carry

    lax.fori_loop(0, nq, qblock, 0)


def _prompt_attn_kernel(qa_ref, ka_ref, va_ref, qb_ref, kbt_ref, vbt_ref, lft_ref, lamv_ref, gs_ref,
                        oa_ref, ob_ref,
                        ka_sc, va_sc, kbt_sc, vbt_sc, cum_sc, m_sc, l_sc, acc_sc, *, tq, tk, lam_init):
    seq = qa_ref.shape[0]
    j = pl.program_id(1)

    @pl.when(j == 0)
    def _():
        carry = jnp.zeros((SUBLANES, 1), F32)
        for c in range(seq // LANES):
            xc = _cumsum_lanes(lft_ref[:, c * LANES:(c + 1) * LANES]) + carry
            cum_sc[:, c * LANES:(c + 1) * LANES] = xc
            carry = xc[:, LANES - 1:LANES]

    ka_sc[...] = ka_ref[...].astype(BF16)
    va_sc[...] = va_ref[...].astype(BF16)
    kbt_sc[...] = kbt_ref[...].astype(BF16)
    vbt_sc[...] = vbt_ref[...].astype(BF16)

    lam = _lam_value(lamv_ref, lam_init)
    gs = gs_ref[...]
    lane = lax.broadcasted_iota(jnp.int32, (tq, LANES), 1)

    def write_a(q0, o1, o2):
        o = o1 - lam * o2
        oa_ref[pl.ds(q0, tq), :] = (_rms(o, gs) * (1.0 - lam_init)).astype(BF16)

    def write_b(q0, o1, o2):
        ob_ref[pl.ds(q0, tq), :] = jnp.where(lane < HEAD_DIM, o1, o2).astype(BF16)

    def bias_b(c, k0):
        return cum_sc[pl.ds(2 * j + c, 1), pl.ds(k0, tk)]

    kw = dict(seq=seq, tq=tq, tk=tk)
    _flash_pair(qa_ref, ka_sc, va_sc, None, write_a, m_sc, l_sc, acc_sc, k_transposed=False, **kw)
    _flash_pair(qb_ref, kbt_sc, vbt_sc, bias_b, write_b, m_sc, l_sc, acc_sc, k_transposed=True, **kw)


def _prompt_attn_call(qa, ka, va, qb, kbt, vbt, lft, lamv, gs, *, lam_init, tq, tk):
    B, T, _ = qa.shape
    n_pair = GROUP_W // LANES
    tokc = pl.BlockSpec((None, T, LANES), lambda b, j: (b, 0, j))
    tr = pl.BlockSpec((None, LANES, T), lambda b, j: (b, j, 0))
    const2 = lambda b, j: (0, 0)
    return pl.pallas_call(
        functools.partial(_prompt_attn_kernel, tq=tq, tk=tk, lam_init=lam_init),
        grid=(B, n_pair),
        in_specs=[tokc, tokc, tokc, tokc, tr, tr,
                  pl.BlockSpec((None, SUBLANES, T), lambda b, j: (b, 0, 0)),
                  pl.BlockSpec((4, HEAD_DIM), const2),
                  pl.BlockSpec((1, A_VDIM), const2)],
        out_specs=[tokc, tokc],
        out_shape=[jax.ShapeDtypeStruct((B, T, GROUP_W), BF16), jax.ShapeDtypeStruct((B, T, GROUP_W), BF16)],
        scratch_shapes=[
            pltpu.VMEM((T, LANES), BF16), pltpu.VMEM((T, LANES), BF16),
            pltpu.VMEM((LANES, T), BF16), pltpu.VMEM((LANES, T), BF16),
            pltpu.VMEM((SUBLANES, T), F32),
            pltpu.VMEM((2, tq, 1), F32), pltpu.VMEM((2, tq, 1), F32), pltpu.VMEM((2, tq, LANES), F32),
        ],
        compiler_params=pltpu.CompilerParams(
            dimension_semantics=("arbitrary", "arbitrary"), vmem_limit_bytes=VMEM_LIMIT),
        name="prompt_attn",
    )(qa, ka, va, qb, kbt, vbt, lft, lamv, gs)


def _sample_attn_kernel(pt_ref, qbda_ref, qbdb_ref, kan_ref, van_ref, kbn_ref, vbn_ref, lfn_ref, lamv_ref, gs_ref,
                        *rest, n_pp, n_a_heads, lam_init):
    pages = rest[:5 * n_pp]
    oa_ref, ob_ref = rest[5 * n_pp:5 * n_pp + 2]
    m_sc, l_sc, acca_sc, accbt_sc, carry_sc = rest[5 * n_pp + 2:]
    xk = pages[0 * n_pp:1 * n_pp]
    xv = pages[1 * n_pp:2 * n_pp]
    kt = pages[2 * n_pp:3 * n_pp]
    vt = pages[3 * n_pp:4 * n_pp]
    lf = pages[4 * n_pp:5 * n_pp]
    g = pl.program_id(1)
    n_rows = 2 * SUBLANES * n_a_heads
    tn = SUBLANES

    @pl.when(g == 0)
    def _():
        m_sc[...] = jnp.full(m_sc.shape, NEG_INF, F32)
        l_sc[...] = jnp.zeros(l_sc.shape, F32)
        acca_sc[...] = jnp.zeros(acca_sc.shape, F32)
        accbt_sc[...] = jnp.zeros(accbt_sc.shape, F32)
        carry_sc[...] = jnp.zeros(carry_sc.shape, F32)

    qbda = qbda_ref[...]
    qbdb = qbdb_ref[...]
    eye = (lax.broadcasted_iota(jnp.int32, (LANES, LANES), 0)
           == lax.broadcasted_iota(jnp.int32, (LANES, LANES), 1)).astype(F32)

    def attend(k_heads, v_heads, ktile, vtile, ck, keep):
        w = ck.shape[1]
        s_rows = [lax.dot_general(qbda[h * 2 * tn:(h + 1) * 2 * tn], k_heads[h], _NT, preferred_element_type=F32)
                  for h in range(n_a_heads)]
        bias = jnp.broadcast_to(ck[:, None, :], (SUBLANES, tn, w)).reshape(SUBLANES * tn, w)
        s_rows.append(jnp.dot(qbdb, ktile, preferred_element_type=F32) - bias)
        s = jnp.concatenate(s_rows, axis=0)
        if keep is not None:
            s = jnp.where(keep, s, NEG_INF)
        m_prev = m_sc[...]
        m_new = jnp.maximum(m_prev, jnp.max(s, axis=1, keepdims=True))
        corr = jnp.exp(m_prev - m_new)
        p = jnp.exp(s - m_new)
        l_sc[...] = corr * l_sc[...] + jnp.sum(p, axis=1, keepdims=True)
        m_sc[...] = m_new
        pb = p.astype(BF16)
        for h in range(n_a_heads):
            r = slice(h * 2 * tn, (h + 1) * 2 * tn)
            acca_sc[r, :] = acca_sc[r, :] * corr[r] + jnp.dot(pb[r], v_heads[h], preferred_element_type=F32)
        corr_row = jnp.sum(eye * corr, axis=0, keepdims=True)
        accbt_sc[...] = accbt_sc[...] * corr_row + lax.dot_general(vtile, pb, _NT, preferred_element_type=F32)

    carry = carry_sc[...]
    cks = []
    for j in range(n_pp):
        xc = _cumsum_lanes(lf[j][...]) + carry
        cks.append(xc)
        carry = xc[:, LANES - 1:LANES]
    page = xk[0].shape[0] // n_a_heads

    def head_rows(refs, h):
        return jnp.concatenate([r[pl.ds(h, page, stride=n_a_heads), :] for r in refs], axis=0).astype(BF16)

    attend([head_rows(xk, h) for h in range(n_a_heads)],
           [head_rows(xv, h) for h in range(n_a_heads)],
           jnp.concatenate([r[...] for r in kt], axis=1).astype(BF16),
           jnp.concatenate([r[...] for r in vt], axis=1).astype(BF16),
           jnp.concatenate(cks, axis=1), None)
    carry_sc[...] = carry

    @pl.when(g == pl.num_programs(1) - 1)
    def _():
        pad = lambda a: jnp.concatenate([a, jnp.zeros((LANES - tn, a.shape[1]), a.dtype)], axis=0)
        kan = kan_ref[...]
        van = van_ref[...]
        ck_new = _cumsum_lanes(lfn_ref[...]) + carry
        rowq = lax.broadcasted_iota(jnp.int32, (LANES, LANES), 0) % tn
        key = lax.broadcasted_iota(jnp.int32, (LANES, LANES), 1)
        attend([pad(kan[:, h * LANES:(h + 1) * LANES]).astype(BF16) for h in range(n_a_heads)],
               [pad(van[:, h * LANES:(h + 1) * LANES]).astype(BF16) for h in range(n_a_heads)],
               pad(kbn_ref[...]).T.astype(BF16), pad(vbn_ref[...]).T.astype(BF16),
               ck_new, key <= rowq)

        lam = _lam_value(lamv_ref, lam_init)
        gs = gs_ref[...]
        inv_l = 1.0 / l_sc[...]
        for h in range(n_a_heads):
            r1 = slice(h * 2 * tn, h * 2 * tn + tn)
            r2 = slice(h * 2 * tn + tn, (h + 1) * 2 * tn)
            o = acca_sc[r1, :] * inv_l[r1] - lam * (acca_sc[r2, :] * inv_l[r2])
            oa_ref[:, h * A_VDIM:(h + 1) * A_VDIM] = (_rms(o, gs) * (1.0 - lam_init)).astype(BF16)
        ob_full = accbt_sc[...].T[n_rows:, :] * inv_l[n_rows:]
        lane = lax.broadcasted_iota(jnp.int32, (tn, LANES), 1)
        for jb in range(GROUP_W // LANES):
            lo = ob_full[(2 * jb) * tn:(2 * jb + 1) * tn, jb * LANES:(jb + 1) * LANES]
            hi = ob_full[(2 * jb + 1) * tn:(2 * jb + 2) * tn, jb * LANES:(jb + 1) * LANES]
            ob_ref[:, jb * LANES:(jb + 1) * LANES] = jnp.where(lane < HEAD_DIM, lo, hi).astype(BF16)


def _sample_attn_call(pt_flat, qbda, qbdb, kan, van, kbn, vbn, lfn, lamv, gs, xk, xv, kt, vt, lf,
                      *, n_pages, n_pp, lam_init):
    Bd = qbda.shape[0]
    n_a_heads = GROUP_W // A_VDIM
    steps = n_pages // n_pp
    per_b = lambda b, g, pt: (b, 0, 0)
    tok = pl.BlockSpec((SUBLANES, GROUP_W), lambda b, g, pt: (b, 0))
    const2 = lambda b, g, pt: (0, 0)

    def page_spec(rows, j):
        return pl.BlockSpec((None, rows, LANES), lambda b, g, pt: (pt[b * n_pages + g * n_pp + j], 0, 0))

    page_specs, page_args = [], []
    for arr in (xk, xv, kt, vt, lf):
        for j in range(n_pp):
            page_specs.append(page_spec(arr.shape[1], j))
            page_args.append(arr)
    grid_spec = pltpu.PrefetchScalarGridSpec(
        num_scalar_prefetch=1,
        grid=(Bd, steps),
        in_specs=[pl.BlockSpec((None, 2 * SUBLANES * n_a_heads, LANES), per_b),
                  pl.BlockSpec((None, SUBLANES * SUBLANES, GROUP_W), per_b),
                  tok, tok, tok, tok,
                  pl.BlockSpec((None, SUBLANES, LANES), per_b),
                  pl.BlockSpec((4, HEAD_DIM), const2),
                  pl.BlockSpec((1, A_VDIM), const2)] + page_specs,
        out_specs=[tok, tok],
        scratch_shapes=[pltpu.VMEM((LANES, 1), F32), pltpu.VMEM((LANES, 1), F32),
                        pltpu.VMEM((2 * SUBLANES * n_a_heads, LANES), F32),
                        pltpu.VMEM((GROUP_W, LANES), F32),
                        pltpu.VMEM((SUBLANES, 1), F32)],
    )
    n_tok = Bd * SUBLANES
    return pl.pallas_call(
        functools.partial(_sample_attn_kernel, n_pp=n_pp, n_a_heads=n_a_heads, lam_init=lam_init),
        grid_spec=grid_spec,
        out_shape=[jax.ShapeDtypeStruct((n_tok, GROUP_W), BF16), jax.ShapeDtypeStruct((n_tok, GROUP_W), BF16)],
        compiler_params=pltpu.CompilerParams(
            dimension_semantics=("arbitrary", "arbitrary"), vmem_limit_bytes=VMEM_LIMIT),
        name="sample_attn",
    )(pt_flat, qbda, qbdb, kan, van, kbn, vbn, lfn, lamv, gs, *page_args)


def _post_kernel(oa_ref, ob_ref, x_ref, wo_ref, g1_ref, g2_ref, wg_ref, wu_ref, wd_ref, g3_ref, y_ref, *, ff_chunk):
    wa = oa_ref.shape[1]
    a = (jnp.dot(oa_ref[...], wo_ref[:wa, :], preferred_element_type=F32)
         + jnp.dot(ob_ref[...], wo_ref[wa:, :], preferred_element_type=F32))
    x1 = x_ref[...] + _rms(a, g1_ref[...])
    hb = _rms(x1, g2_ref[...]).astype(BF16)
    d_ff = wg_ref.shape[1]
    d = None
    for c in range(d_ff // ff_chunk):
        cs = slice(c * ff_chunk, (c + 1) * ff_chunk)
        gate = jnp.dot(hb, wg_ref[:, cs], preferred_element_type=F32)
        up = jnp.dot(hb, wu_ref[:, cs], preferred_element_type=F32)
        f = (gate * jax.nn.sigmoid(gate) * up).astype(BF16)
        part = jnp.dot(f, wd_ref[cs, :], preferred_element_type=F32)
        d = part if d is None else d + part
    y_ref[...] = x1 + _rms(d, g3_ref[...])


def _post_call(oa, ob, x2, wo, g1, g2, wg, wu, wd, g3, *, tm, ff_chunk):
    N, D = x2.shape
    row = lambda i: (i, 0)
    const = lambda i: (0, 0)
    resident = lambda a: pl.BlockSpec(a.shape, const, pipeline_mode=pl.Buffered(1))
    vec = pl.BlockSpec((1, D), const)
    return pl.pallas_call(
        functools.partial(_post_kernel, ff_chunk=ff_chunk),
        grid=(N // tm,),
        in_specs=[pl.BlockSpec((tm, GROUP_W), row), pl.BlockSpec((tm, GROUP_W), row), pl.BlockSpec((tm, D), row),
                  resident(wo), vec, vec, resident(wg), resident(wu), resident(wd), vec],
        out_specs=pl.BlockSpec((tm, D), row),
        out_shape=jax.ShapeDtypeStruct((N, D), F32),
        compiler_params=pltpu.CompilerParams(
            dimension_semantics=("arbitrary",), vmem_limit_bytes=VMEM_LIMIT),
        name="post",
    )(oa, ob, x2, wo, g1, g2, wg, wu, wd, g3)


def _rope_tables(pos):
    half = HEAD_DIM // 2
    inv = ROPE_THETA ** (-jnp.arange(half, dtype=F32) / half)
    ang = pos[:, None] * inv[None, :]
    cos = jnp.tile(jnp.cos(ang), (1, LANES // half))
    sign = jnp.where((jnp.arange(LANES) % HEAD_DIM) < half, -1.0, 1.0).astype(F32)
    sin = jnp.tile(jnp.sin(ang), (1, LANES // half)) * sign[None, :]
    return cos, sin


def _pick(n, prefs):
    for p in prefs:
        if n % p == 0:
            return p
    return n


def kernel(x_prompt, x_sample, cache_a_k, cache_a_v, cache_b_k, cache_b_v, cache_b_logf, page_table,
           g_pre_mix, w_in, b_f, lam_q1, lam_k1, lam_q2, lam_k2, g_subln, w_out, g_post_mix,
           g_pre_ffn, w_gate, w_up, w_down, g_post_ffn):
    B, T, D = x_prompt.shape
    Bd, Tn, _ = x_sample.shape
    depth = w_in.shape[0]
    n_pool, page = cache_a_k.shape[1], cache_a_k.shape[2]
    n_pages = page_table.shape[1]
    past_len = n_pages * page
    d_ff = w_gate.shape[2]
    n_b_heads = cache_b_k.shape[3]
    in_cols = w_in.shape[2]
    assert D == 2 * GROUP_W and Tn == SUBLANES and page == LANES and n_b_heads == SUBLANES
    assert in_cols == 6 * GROUP_W + n_b_heads

    tm_p = _pick(T, (512, 256, 128))
    n_tok_s = Bd * Tn
    tm_s = _pick(n_tok_s, (512, 256, 128, 8))
    tq = _pick(T, (512, 256, 128))
    tk = _pick(tq, (256, 128))
    n_pp = _pick(n_pages, (8, 4, 2, 1))
    ff_chunk = _pick(d_ff, (1408, 1024, 512, 256, 128))

    cos_p, sin_p = _rope_tables(jnp.arange(T, dtype=F32))
    pos_s = past_len + jnp.arange(Tn, dtype=F32)
    cos_s, sin_s = _rope_tables(jnp.tile(pos_s, tm_s // Tn))

    pt_flat = page_table.reshape(-1).astype(jnp.int32)
    yp = x_prompt
    ys = x_sample.reshape(1, n_tok_s, D)
    rows_p, rows_s = [], []
    for layer in range(depth):
        lam_init = _lambda_init(layer)
        w_bf = jnp.pad(w_in[layer].astype(BF16), ((0, 0), (0, 6 * GROUP_W + LANES - in_cols)))
        bfp = jnp.pad(b_f[layer], (0, LANES - n_b_heads)).reshape(1, LANES)
        g0 = g_pre_mix[layer].reshape(1, D)
        lamv = jnp.stack([lam_q1[layer], lam_k1[layer], lam_q2[layer], lam_k2[layer]]).astype(F32)
        gs = g_subln[layer].reshape(1, A_VDIM)
        post_w = (w_out[layer].astype(BF16), g_post_mix[layer].reshape(1, D), g_pre_ffn[layer].reshape(1, D),
                  w_gate[layer].astype(BF16), w_up[layer].astype(BF16), w_down[layer].astype(BF16),
                  g_post_ffn[layer].reshape(1, D))

        qa, ka, va, qb, kbt, vbt, lft = _proj_call(yp, g0, w_bf, bfp, cos_p, sin_p, tm=tm_p, transpose_b=True)
        oa, ob = _prompt_attn_call(qa, ka, va, qb, kbt, vbt, lft, lamv, gs, lam_init=lam_init, tq=tq, tk=tk)
        yp = _post_call(oa.reshape(B * T, GROUP_W), ob.reshape(B * T, GROUP_W), yp.reshape(B * T, D), *post_w,
                        tm=tm_p, ff_chunk=ff_chunk).reshape(B, T, D)
        n_a_heads = GROUP_W // A_VDIM
        rows_p.append((ka.reshape(B, T, n_a_heads, A_VDIM), va.reshape(B, T, n_a_heads, A_VDIM),
                       kbt.reshape(B, n_b_heads, HEAD_DIM, T).transpose(0, 3, 1, 2),
                       vbt.reshape(B, n_b_heads, HEAD_DIM, T).transpose(0, 3, 1, 2),
                       lft.transpose(0, 2, 1)))

        qa_s, ka_s, va_s, qb_s, kb_s, vb_s, lft_s = _proj_call(ys, g0, w_bf, bfp, cos_s, sin_s, tm=tm_s,
                                                               transpose_b=False)
        qa5 = qa_s.reshape(Bd, Tn, n_a_heads, 2, HEAD_DIM).transpose(0, 2, 3, 1, 4)
        qbda = (qa5[:, :, :, :, None, :] * jnp.eye(2, dtype=BF16)[None, None, :, None, :, None]
                ).reshape(Bd, n_a_heads * 2 * Tn, A_VDIM)
        qb4 = qb_s.reshape(Bd, Tn, n_b_heads, HEAD_DIM).transpose(0, 2, 1, 3)
        qbdb = (qb4[:, :, :, None, :] * jnp.eye(n_b_heads, dtype=BF16)[None, :, None, :, None]
                ).reshape(Bd, n_b_heads * Tn, GROUP_W)
        lfn = jnp.pad(lft_s[0].reshape(n_b_heads, Bd, Tn).transpose(1, 0, 2), ((0, 0), (0, 0), (0, LANES - Tn)))
        xk = cache_a_k[layer].reshape(n_pool, page * n_a_heads, A_VDIM)
        xv = cache_a_v[layer].reshape(n_pool, page * n_a_heads, A_VDIM)
        kt = cache_b_k[layer].transpose(0, 2, 3, 1).reshape(n_pool, n_b_heads * HEAD_DIM, page)
        vt = cache_b_v[layer].transpose(0, 2, 3, 1).reshape(n_pool, n_b_heads * HEAD_DIM, page)
        lf = cache_b_logf[layer].transpose(0, 2, 1)
        oa_s, ob_s = _sample_attn_call(pt_flat, qbda, qbdb, ka_s[0], va_s[0], kb_s[0], vb_s[0], lfn, lamv, gs,
                                       xk, xv, kt, vt, lf, n_pages=n_pages, n_pp=n_pp, lam_init=lam_init)
        ys = _post_call(oa_s, ob_s, ys[0], *post_w, tm=tm_s, ff_chunk=ff_chunk).reshape(1, n_tok_s, D)
        rows_s.append((ka_s.reshape(Bd, Tn, n_a_heads, A_VDIM), va_s.reshape(Bd, Tn, n_a_heads, A_VDIM),
                       kb_s.reshape(Bd, Tn, n_b_heads, HEAD_DIM), vb_s.reshape(Bd, Tn, n_b_heads, HEAD_DIM),
                       lft_s[0].reshape(n_b_heads, Bd, Tn).transpose(1, 2, 0)))

    stack = lambda rows, i: jnp.stack([r[i] for r in rows], axis=0)
    return (yp, ys.reshape(Bd, Tn, D),
            stack(rows_p, 0), stack(rows_p, 1), stack(rows_p, 2), stack(rows_p, 3), stack(rows_p, 4),
            stack(rows_s, 0), stack(rows_s, 1), stack(rows_s, 2), stack(rows_s, 3), stack(rows_s, 4))
```

```python
import functools
import math

import jax
import jax.numpy as jnp
from jax import lax
from jax.experimental import pallas as pl
from jax.experimental.pallas import tpu as pltpu

F32 = jnp.float32
BF16 = jnp.bfloat16

HEAD_DIM = 64
A_VDIM = 2 * HEAD_DIM
ROPE_THETA = 10000.0
RMS_EPS = 1e-6
NEG_INF = -1e30
ATTN_SCALE = HEAD_DIM ** -0.5

LANES = 128
SUBLANES = 8
GROUP_W = 512
VMEM_LIMIT = 56 * 1024 * 1024


def _lambda_init(layer):
    return 0.8 - 0.6 * math.exp(-0.3 * layer)


def _rms(x, g):
    return x * lax.rsqrt(jnp.mean(x * x, axis=-1, keepdims=True) + RMS_EPS) * g


def _lam_value(lamv_ref, lam_init):
    v = lamv_ref[...]
    a = jnp.sum(v[0:1] * v[1:2], axis=1, keepdims=True)
    b = jnp.sum(v[2:3] * v[3:4], axis=1, keepdims=True)
    return jnp.exp(a) - jnp.exp(b) + lam_init


def _cumsum_lanes(x):
    lane = lax.broadcasted_iota(jnp.int32, x.shape, 1)
    for s in (1, 2, 4, 8, 16, 32, 64):
        x = x + jnp.where(lane >= s, pltpu.roll(x, s, 1), 0.0)
    return x


def _proj_kernel(x_ref, g_ref, w_ref, bf_ref, cos_ref, sin_ref, *outs, prompt):
    if prompt:
        qa_ref, ka_ref, va_ref, kab_ref, vab_ref, qb_ref, kb_ref, vb_ref, kbb_ref, vbb_ref, lft_ref = outs
    else:
        qa_ref, ka_ref, va_ref, qb_ref, kb_ref, vb_ref, lft_ref = outs
    n_a_heads = GROUP_W // A_VDIM
    x = x_ref[...]
    tm = x.shape[0]
    hb = _rms(x, g_ref[...]).astype(BF16)
    cos4 = jnp.concatenate([cos_ref[...]] * 4, axis=1)
    sin4 = jnp.concatenate([sin_ref[...]] * 4, axis=1)
    lane = lax.broadcasted_iota(jnp.int32, (tm, GROUP_W), 1)
    first_half = (lane % HEAD_DIM) < (HEAD_DIM // 2)

    def col(j, width=GROUP_W):
        return jnp.dot(hb, w_ref[:, j * GROUP_W:j * GROUP_W + width], preferred_element_type=F32)

    def rope(y):
        partner = jnp.where(first_half, pltpu.roll(y, GROUP_W - HEAD_DIM // 2, 1),
                            pltpu.roll(y, HEAD_DIM // 2, 1))
        return y * cos4 + partner * sin4

    def store_rows_by_head(ref, y):
        for h in range(n_a_heads):
            ref[pl.ds(h, tm, stride=n_a_heads), :] = y[:, h * A_VDIM:(h + 1) * A_VDIM]

    qa_ref[...] = (rope(col(0)) * ATTN_SCALE).astype(BF16)
    ka = rope(col(1))
    va = col(2)
    store_rows_by_head(ka_ref, ka)
    store_rows_by_head(va_ref, va)
    qb_ref[...] = (col(3) * ATTN_SCALE).astype(BF16)
    kb = col(4)
    vb = col(5)
    if prompt:
        kab_ref[...] = ka.astype(BF16)
        vab_ref[...] = va.astype(BF16)
        kbt = kb.T
        vbt = vb.T
        kb_ref[...] = kbt
        vb_ref[...] = vbt
        kbb_ref[...] = kbt.astype(BF16)
        vbb_ref[...] = vbt.astype(BF16)
    else:
        kb_ref[...] = kb
        vb_ref[...] = vb
    fl = col(6, LANES) + bf_ref[...]
    lf = jnp.minimum(fl, 0.0) - jnp.log1p(jnp.exp(-jnp.abs(fl)))
    lft_ref[...] = lf.T[:SUBLANES, :]


def _proj_call(x3, g, w_bf, bfp, cos_t, sin_t, *, tm, prompt):
    B, T, D = x3.shape
    n_a_heads = GROUP_W // A_VDIM
    nt = T // tm
    ntab = cos_t.shape[0] // tm
    row = lambda b, i: (b, i, 0)
    colm = lambda b, i: (b, 0, i)
    const = lambda b, i: (0, 0)
    tab = lambda b, i: (i % ntab, 0)
    tok_spec = pl.BlockSpec((None, tm, GROUP_W), row)
    tok = lambda dt: jax.ShapeDtypeStruct((B, T, GROUP_W), dt)
    il_spec = pl.BlockSpec((None, tm * n_a_heads, A_VDIM), row)
    il = jax.ShapeDtypeStruct((B, T * n_a_heads, A_VDIM), F32)
    tr_spec = pl.BlockSpec((None, GROUP_W, tm), colm)
    tr = lambda dt: jax.ShapeDtypeStruct((B, GROUP_W, T), dt)
    lft_spec = pl.BlockSpec((None, SUBLANES, tm), colm)
    lft = jax.ShapeDtypeStruct((B, SUBLANES, T), F32)
    if prompt:
        out_specs = [tok_spec, il_spec, il_spec, tok_spec, tok_spec, tok_spec, tr_spec, tr_spec, tr_spec, tr_spec,
                     lft_spec]
        out_shape = [tok(BF16), il, il, tok(BF16), tok(BF16), tok(BF16), tr(F32), tr(F32), tr(BF16), tr(BF16), lft]
    else:
        out_specs = [tok_spec, il_spec, il_spec, tok_spec, tok_spec, tok_spec, lft_spec]
        out_shape = [tok(BF16), il, il, tok(BF16), tok(F32), tok(F32), lft]
    return pl.pallas_call(
        functools.partial(_proj_kernel, prompt=prompt),
        grid=(B, nt),
        in_specs=[
            pl.BlockSpec((None, tm, D), row),
            pl.BlockSpec((1, D), const),
            pl.BlockSpec(w_bf.shape, const),
            pl.BlockSpec((1, LANES), const),
            pl.BlockSpec((tm, LANES), tab),
            pl.BlockSpec((tm, LANES), tab),
        ],
        out_specs=out_specs,
        out_shape=out_shape,
        compiler_params=pltpu.CompilerParams(
            dimension_semantics=("arbitrary", "arbitrary"), vmem_limit_bytes=VMEM_LIMIT),
        name="proj",
    )(x3, g, w_bf, bfp, cos_t, sin_t)


_NT = (((1,), (1,)), ((), ()))


def _flash_pair(q_ref, k_ref, vext_sc, bias_row, write_out, m_sc, acc_sc, *, seq, tq, k_transposed):
    nq = seq // tq
    half = tq // 2
    lane = lax.broadcasted_iota(jnp.int32, (tq, LANES), 1)

    def qblock(i, carry):
        q0 = pl.multiple_of(i * tq, tq)
        q = q_ref[pl.ds(q0, tq), :]
        zero = jnp.zeros_like(q)
        qs = (jnp.where(lane < HEAD_DIM, q, zero), jnp.where(lane >= HEAD_DIM, q, zero))
        m_sc[...] = jnp.full(m_sc.shape, NEG_INF, F32)
        acc_sc[...] = jnp.zeros(acc_sc.shape, F32)

        def block(r0, nr, k0, kw, masked):
            if k_transposed:
                k = k_ref[:, pl.ds(k0, kw)]
                v = vext_sc[:, pl.ds(k0, kw)]
            else:
                k = k_ref[pl.ds(k0, kw), :]
                v = vext_sc[pl.ds(k0, kw), :]
            if masked:
                rowg = q0 + r0 + lax.broadcasted_iota(jnp.int32, (nr, kw), 0)
                colg = k0 + lax.broadcasted_iota(jnp.int32, (nr, kw), 1)
                keep = colg <= rowg
            for c in range(2):
                qc = qs[c][r0:r0 + nr]
                if k_transposed:
                    s = jnp.dot(qc, k, preferred_element_type=F32)
                else:
                    s = lax.dot_general(qc, k, _NT, preferred_element_type=F32)
                if bias_row is not None:
                    s = s - bias_row(c, k0, kw)
                if masked:
                    s = jnp.where(keep, s, NEG_INF)
                m_prev = m_sc[c, r0:r0 + nr, :]
                m_new = jnp.maximum(m_prev, jnp.max(s, axis=1, keepdims=True))
                corr = jnp.exp(m_prev - m_new)
                p = jnp.exp(s - jnp.concatenate([m_new] * (kw // LANES), axis=1))
                pb = p.astype(BF16)
                if k_transposed:
                    pv = lax.dot_general(pb, v, _NT, preferred_element_type=F32)
                else:
                    pv = jnp.dot(pb, v, preferred_element_type=F32)
                acc_sc[c, r0:r0 + nr, :] = acc_sc[c, r0:r0 + nr, :] * jnp.concatenate([corr, corr], axis=1) + pv
                m_sc[c, r0:r0 + nr, :] = m_new

        def full_block(j, c):
            block(0, tq, pl.multiple_of(j * tq, tq), tq, False)
            return c

        lax.fori_loop(0, i, full_block, 0)
        block(0, tq, q0, half, True)
        block(half, half, pl.multiple_of(q0 + half, half), half, True)
        acc0 = acc_sc[0]
        acc1 = acc_sc[1]
        write_out(q0, acc0[:, :LANES] / acc0[:, LANES:], acc1[:, :LANES] / acc1[:, LANES:])
        return carry

    lax.fori_loop(0, nq, qblock, 0)


def _prompt_attn_kernel(qa_ref, ka_ref, va_ref, qb_ref, kbt_ref, vbt_ref, lft_ref, lamv_ref, gs_ref,
                        oa_ref, ob_ref,
                        vaext_sc, vbext_sc, cum_sc, m_sc, acc_sc, *, tq, lam_init):
    seq = qa_ref.shape[0]
    j = pl.program_id(1)

    @pl.when(j == 0)
    def _():
        carry = jnp.zeros((SUBLANES, LANES), F32)
        for c in range(seq // LANES):
            x = lft_ref[:, c * LANES:(c + 1) * LANES]
            cum_sc[:, c * LANES:(c + 1) * LANES] = _cumsum_lanes(x) + carry
            carry = carry + jnp.sum(x, axis=1, keepdims=True)

    vaext_sc[:, :LANES] = va_ref[...]
    vaext_sc[:, LANES:] = jnp.ones((seq, LANES), BF16)
    vbext_sc[:LANES, :] = vbt_ref[...]
    vbext_sc[LANES:, :] = jnp.ones((LANES, seq), BF16)

    lam = _lam_value(lamv_ref, lam_init)
    gs = gs_ref[...]
    lane = lax.broadcasted_iota(jnp.int32, (tq, LANES), 1)

    def write_a(q0, o1, o2):
        o = o1 - lam * o2
        oa_ref[pl.ds(q0, tq), :] = (_rms(o, gs) * (1.0 - lam_init)).astype(BF16)

    def write_b(q0, o1, o2):
        ob_ref[pl.ds(q0, tq), :] = jnp.where(lane < HEAD_DIM, o1, o2).astype(BF16)

    def bias_b(c, k0, kw):
        return cum_sc[pl.ds(2 * j + c, 1), pl.ds(k0, kw)]

    _flash_pair(qa_ref, ka_ref, vaext_sc, None, write_a, m_sc, acc_sc, seq=seq, tq=tq, k_transposed=False)
    _flash_pair(qb_ref, kbt_ref, vbext_sc, bias_b, write_b, m_sc, acc_sc, seq=seq, tq=tq, k_transposed=True)


def _prompt_attn_call(qa, ka, va, qb, kbt, vbt, lft, lamv, gs, *, lam_init, tq):
    B, T, _ = qa.shape
    n_pair = GROUP_W // LANES
    tokc = pl.BlockSpec((None, T, LANES), lambda b, j: (b, 0, j))
    tr = pl.BlockSpec((None, LANES, T), lambda b, j: (b, j, 0))
    const2 = lambda b, j: (0, 0)
    return pl.pallas_call(
        functools.partial(_prompt_attn_kernel, tq=tq, lam_init=lam_init),
        grid=(B, n_pair),
        in_specs=[tokc, tokc, tokc, tokc, tr, tr,
                  pl.BlockSpec((None, SUBLANES, T), lambda b, j: (b, 0, 0)),
                  pl.BlockSpec((4, HEAD_DIM), const2),
                  pl.BlockSpec((1, A_VDIM), const2)],
        out_specs=[tokc, tokc],
        out_shape=[jax.ShapeDtypeStruct((B, T, GROUP_W), BF16), jax.ShapeDtypeStruct((B, T, GROUP_W), BF16)],
        scratch_shapes=[
            pltpu.VMEM((T, 2 * LANES), BF16), pltpu.VMEM((2 * LANES, T), BF16),
            pltpu.VMEM((SUBLANES, T), F32),
            pltpu.VMEM((2, tq, LANES), F32), pltpu.VMEM((2, tq, 2 * LANES), F32),
        ],
        compiler_params=pltpu.CompilerParams(
            dimension_semantics=("arbitrary", "arbitrary"), vmem_limit_bytes=VMEM_LIMIT),
        name="prompt_attn",
    )(qa, ka, va, qb, kbt, vbt, lft, lamv, gs)


def _sample_attn_kernel(pt_ref, qbda_ref, qbdb_ref, kan_ref, van_ref, kbn_ref, vbn_ref, lfn_ref, lamv_ref, gs_ref,
                        *rest, n_pp, n_a_heads, lam_init):
    pages = rest[:5 * n_pp]
    oa_ref, ob_ref = rest[5 * n_pp:5 * n_pp + 2]
    m_sc, l_sc, acca_sc, accbt_sc, carry_sc = rest[5 * n_pp + 2:]
    xk = pages[0 * n_pp:1 * n_pp]
    xv = pages[1 * n_pp:2 * n_pp]
    kt = pages[2 * n_pp:3 * n_pp]
    vt = pages[3 * n_pp:4 * n_pp]
    lf = pages[4 * n_pp:5 * n_pp]
    g = pl.program_id(1)
    n_rows = 2 * SUBLANES * n_a_heads
    tn = SUBLANES

    @pl.when(g == 0)
    def _():
        m_sc[...] = jnp.full(m_sc.shape, NEG_INF, F32)
        l_sc[...] = jnp.zeros(l_sc.shape, F32)
        acca_sc[...] = jnp.zeros(acca_sc.shape, F32)
        accbt_sc[...] = jnp.zeros(accbt_sc.shape, F32)
        carry_sc[...] = jnp.zeros(carry_sc.shape, F32)

    qbda = qbda_ref[...]
    qbdb = qbdb_ref[...]
    eye = (lax.broadcasted_iota(jnp.int32, (LANES, LANES), 0)
           == lax.broadcasted_iota(jnp.int32, (LANES, LANES), 1)).astype(F32)

    def attend(k_heads, v_heads, ktile, vtile, ck, keep):
        w = ck.shape[1]
        s_rows = [lax.dot_general(qbda[h * 2 * tn:(h + 1) * 2 * tn], k_heads[h], _NT, preferred_element_type=F32)
                  for h in range(n_a_heads)]
        bias = jnp.broadcast_to(ck[:, None, :], (SUBLANES, tn, w)).reshape(SUBLANES * tn, w)
        s_rows.append(jnp.dot(qbdb, ktile, preferred_element_type=F32) - bias)
        s = jnp.concatenate(s_rows, axis=0)
        if keep is not None:
            s = jnp.where(keep, s, NEG_INF)
        m_prev = m_sc[...]
        m_new = jnp.maximum(m_prev, jnp.max(s, axis=1, keepdims=True))
        corr = jnp.exp(m_prev - m_new)
        p = jnp.exp(s - jnp.concatenate([m_new] * (w // LANES), axis=1))
        l_sc[...] = corr * l_sc[...] + jnp.sum(p, axis=1, keepdims=True)
        m_sc[...] = m_new
        pb = p.astype(BF16)
        for h in range(n_a_heads):
            r = slice(h * 2 * tn, (h + 1) * 2 * tn)
            acca_sc[r, :] = acca_sc[r, :] * corr[r] + jnp.dot(pb[r], v_heads[h], preferred_element_type=F32)
        corr_row = jnp.sum(eye * corr, axis=0, keepdims=True)
        accbt_sc[...] = accbt_sc[...] * corr_row + lax.dot_general(vtile, pb, _NT, preferred_element_type=F32)

    carry = carry_sc[...]
    cks = []
    for j in range(n_pp):
        x = lf[j][...]
        cks.append(_cumsum_lanes(x) + carry)
        carry = carry + jnp.sum(x, axis=1, keepdims=True)
    page = xk[0].shape[0] // n_a_heads

    def head_rows(refs, h):
        return jnp.concatenate([r[pl.ds(h, page, stride=n_a_heads), :] for r in refs], axis=0).astype(BF16)

    attend([head_rows(xk, h) for h in range(n_a_heads)],
           [head_rows(xv, h) for h in range(n_a_heads)],
           jnp.concatenate([r[...] for r in kt], axis=1).astype(BF16),
           jnp.concatenate([r[...] for r in vt], axis=1).astype(BF16),
           jnp.concatenate(cks, axis=1), None)
    carry_sc[...] = carry

    @pl.when(g == pl.num_programs(1) - 1)
    def _():
        pad = lambda a: jnp.concatenate([a, jnp.zeros((LANES - tn, a.shape[1]), a.dtype)], axis=0)
        new_head = lambda ref, h: pad(ref[pl.ds(h, tn, stride=n_a_heads), :]).astype(BF16)
        ck_new = _cumsum_lanes(lfn_ref[...]) + carry
        rowq = lax.broadcasted_iota(jnp.int32, (LANES, LANES), 0) % tn
        key = lax.broadcasted_iota(jnp.int32, (LANES, LANES), 1)
        attend([new_head(kan_ref, h) for h in range(n_a_heads)],
               [new_head(van_ref, h) for h in range(n_a_heads)],
               pad(kbn_ref[...]).T.astype(BF16), pad(vbn_ref[...]).T.astype(BF16),
               ck_new, key <= rowq)

        lam = _lam_value(lamv_ref, lam_init)
        gs = gs_ref[...]
        inv_l = 1.0 / l_sc[...]
        for h in range(n_a_heads):
            r1 = slice(h * 2 * tn, h * 2 * tn + tn)
            r2 = slice(h * 2 * tn + tn, (h + 1) * 2 * tn)
            o = acca_sc[r1, :] * inv_l[r1] - lam * (acca_sc[r2, :] * inv_l[r2])
            oa_ref[:, h * A_VDIM:(h + 1) * A_VDIM] = (_rms(o, gs) * (1.0 - lam_init)).astype(BF16)
        inv_lb = jnp.concatenate([inv_l[n_rows:]] * (GROUP_W // LANES), axis=1)
        ob_full = accbt_sc[...].T[n_rows:, :] * inv_lb
        lane = lax.broadcasted_iota(jnp.int32, (tn, LANES), 1)
        for jb in range(GROUP_W // LANES):
            lo = ob_full[(2 * jb) * tn:(2 * jb + 1) * tn, jb * LANES:(jb + 1) * LANES]
            hi = ob_full[(2 * jb + 1) * tn:(2 * jb + 2) * tn, jb * LANES:(jb + 1) * LANES]
            ob_ref[:, jb * LANES:(jb + 1) * LANES] = jnp.where(lane < HEAD_DIM, lo, hi).astype(BF16)


def _sample_attn_call(pt_flat, qbda, qbdb, kan, van, kbn, vbn, lfn, lamv, gs, xk, xv, kt, vt, lf,
                      *, n_pages, n_pp, lam_init):
    Bd = qbda.shape[0]
    n_a_heads = GROUP_W // A_VDIM
    steps = n_pages // n_pp
    per_b = lambda b, g, pt: (b, 0, 0)
    tok = pl.BlockSpec((SUBLANES, GROUP_W), lambda b, g, pt: (b, 0))
    tok_il = pl.BlockSpec((SUBLANES * n_a_heads, A_VDIM), lambda b, g, pt: (b, 0))
    const2 = lambda b, g, pt: (0, 0)

    def page_spec(rows, j):
        return pl.BlockSpec((None, rows, LANES), lambda b, g, pt: (pt[b * n_pages + g * n_pp + j], 0, 0))

    page_specs, page_args = [], []
    for arr in (xk, xv, kt, vt, lf):
        for j in range(n_pp):
            page_specs.append(page_spec(arr.shape[1], j))
            page_args.append(arr)
    grid_spec = pltpu.PrefetchScalarGridSpec(
        num_scalar_prefetch=1,
        grid=(Bd, steps),
        in_specs=[pl.BlockSpec((None, 2 * SUBLANES * n_a_heads, LANES), per_b),
                  pl.BlockSpec((None, SUBLANES * SUBLANES, GROUP_W), per_b),
                  tok_il, tok_il, tok, tok,
                  pl.BlockSpec((None, SUBLANES, LANES), per_b),
                  pl.BlockSpec((4, HEAD_DIM), const2),
                  pl.BlockSpec((1, A_VDIM), const2)] + page_specs,
        out_specs=[tok, tok],
        scratch_shapes=[pltpu.VMEM((LANES, LANES), F32), pltpu.VMEM((LANES, LANES), F32),
                        pltpu.VMEM((2 * SUBLANES * n_a_heads, LANES), F32),
                        pltpu.VMEM((GROUP_W, LANES), F32),
                        pltpu.VMEM((SUBLANES, LANES), F32)],
    )
    n_tok = Bd * SUBLANES
    return pl.pallas_call(
        functools.partial(_sample_attn_kernel, n_pp=n_pp, n_a_heads=n_a_heads, lam_init=lam_init),
        grid_spec=grid_spec,
        out_shape=[jax.ShapeDtypeStruct((n_tok, GROUP_W), BF16), jax.ShapeDtypeStruct((n_tok, GROUP_W), BF16)],
        compiler_params=pltpu.CompilerParams(
            dimension_semantics=("arbitrary", "arbitrary"), vmem_limit_bytes=VMEM_LIMIT),
        name="sample_attn",
    )(pt_flat, qbda, qbdb, kan, van, kbn, vbn, lfn, lamv, gs, *page_args)


def _post_kernel(oa_ref, ob_ref, x_ref, wo_ref, g1_ref, g2_ref, wg_ref, wu_ref, wd_ref, g3_ref, y_ref, *, ff_chunk):
    wa = oa_ref.shape[1]
    a = (jnp.dot(oa_ref[...], wo_ref[:wa, :], preferred_element_type=F32)
         + jnp.dot(ob_ref[...], wo_ref[wa:, :], preferred_element_type=F32))
    x1 = x_ref[...] + _rms(a, g1_ref[...])
    hb = _rms(x1, g2_ref[...]).astype(BF16)
    d_ff = wg_ref.shape[1]
    d = None
    for c in range(d_ff // ff_chunk):
        cs = slice(c * ff_chunk, (c + 1) * ff_chunk)
        gate = jnp.dot(hb, wg_ref[:, cs], preferred_element_type=F32)
        up = jnp.dot(hb, wu_ref[:, cs], preferred_element_type=F32)
        f = (gate * jax.nn.sigmoid(gate) * up).astype(BF16)
        part = jnp.dot(f, wd_ref[cs, :], preferred_element_type=F32)
        d = part if d is None else d + part
    y_ref[...] = x1 + _rms(d, g3_ref[...])


def _post_call(oa, ob, x2, wo, g1, g2, wg, wu, wd, g3, *, tm, ff_chunk):
    N, D = x2.shape
    row = lambda i: (i, 0)
    const = lambda i: (0, 0)
    resident = lambda a: pl.BlockSpec(a.shape, const, pipeline_mode=pl.Buffered(1))
    vec = pl.BlockSpec((1, D), const)
    return pl.pallas_call(
        functools.partial(_post_kernel, ff_chunk=ff_chunk),
        grid=(N // tm,),
        in_specs=[pl.BlockSpec((tm, GROUP_W), row), pl.BlockSpec((tm, GROUP_W), row), pl.BlockSpec((tm, D), row),
                  resident(wo), vec, vec, resident(wg), resident(wu), resident(wd), vec],
        out_specs=pl.BlockSpec((tm, D), row),
        out_shape=jax.ShapeDtypeStruct((N, D), F32),
        compiler_params=pltpu.CompilerParams(
            dimension_semantics=("arbitrary",), vmem_limit_bytes=VMEM_LIMIT),
        name="post",
    )(oa, ob, x2, wo, g1, g2, wg, wu, wd, g3)


def _rope_tables(pos):
    half = HEAD_DIM // 2
    inv = ROPE_THETA ** (-jnp.arange(half, dtype=F32) / half)
    ang = pos[:, None] * inv[None, :]
    cos = jnp.tile(jnp.cos(ang), (1, LANES // half))
    sign = jnp.where((jnp.arange(LANES) % HEAD_DIM) < half, -1.0, 1.0).astype(F32)
    sin = jnp.tile(jnp.sin(ang), (1, LANES // half)) * sign[None, :]
    return cos, sin


def _pick(n, prefs):
    for p in prefs:
        if n % p == 0:
            return p
    return n


def kernel(x_prompt, x_sample, cache_a_k, cache_a_v, cache_b_k, cache_b_v, cache_b_logf, page_table,
           g_pre_mix, w_in, b_f, lam_q1, lam_k1, lam_q2, lam_k2, g_subln, w_out, g_post_mix,
           g_pre_ffn, w_gate, w_up, w_down, g_post_ffn):
    B, T, D = x_prompt.shape
    Bd, Tn, _ = x_sample.shape
    depth = w_in.shape[0]
    n_pool, page = cache_a_k.shape[1], cache_a_k.shape[2]
    n_pages = page_table.shape[1]
    past_len = n_pages * page
    d_ff = w_gate.shape[2]
    n_b_heads = cache_b_k.shape[3]
    in_cols = w_in.shape[2]
    assert D == 2 * GROUP_W and Tn == SUBLANES and page == LANES and n_b_heads == SUBLANES
    assert in_cols == 6 * GROUP_W + n_b_heads

    tm_p = _pick(T, (512, 256, 128))
    n_tok_s = Bd * Tn
    tm_s = _pick(n_tok_s, (512, 256, 128, 8))
    tq = _pick(T, (512, 256, 128))
    n_pp = _pick(n_pages, (8, 4, 2, 1))
    ff_chunk = _pick(d_ff, (1408, 1024, 512, 256, 128))

    cos_p, sin_p = _rope_tables(jnp.arange(T, dtype=F32))
    pos_s = past_len + jnp.arange(Tn, dtype=F32)
    cos_s, sin_s = _rope_tables(jnp.tile(pos_s, tm_s // Tn))

    pt_flat = page_table.reshape(-1).astype(jnp.int32)
    yp = x_prompt
    ys = x_sample.reshape(1, n_tok_s, D)
    rows_p, rows_s = [], []
    for layer in range(depth):
        lam_init = _lambda_init(layer)
        w_bf = jnp.pad(w_in[layer].astype(BF16), ((0, 0), (0, 6 * GROUP_W + LANES - in_cols)))
        bfp = jnp.pad(b_f[layer], (0, LANES - n_b_heads)).reshape(1, LANES)
        g0 = g_pre_mix[layer].reshape(1, D)
        lamv = jnp.stack([lam_q1[layer], lam_k1[layer], lam_q2[layer], lam_k2[layer]]).astype(F32)
        gs = g_subln[layer].reshape(1, A_VDIM)
        post_w = (w_out[layer].astype(BF16), g_post_mix[layer].reshape(1, D), g_pre_ffn[layer].reshape(1, D),
                  w_gate[layer].astype(BF16), w_up[layer].astype(BF16), w_down[layer].astype(BF16),
                  g_post_ffn[layer].reshape(1, D))

        qa, ka, va, ka_bf, va_bf, qb, kbt, vbt, kbt_bf, vbt_bf, lft = _proj_call(
            yp, g0, w_bf, bfp, cos_p, sin_p, tm=tm_p, prompt=True)
        oa, ob = _prompt_attn_call(qa, ka_bf, va_bf, qb, kbt_bf, vbt_bf, lft, lamv, gs, lam_init=lam_init, tq=tq)
        yp = _post_call(oa.reshape(B * T, GROUP_W), ob.reshape(B * T, GROUP_W), yp.reshape(B * T, D), *post_w,
                        tm=tm_p, ff_chunk=ff_chunk).reshape(B, T, D)
        n_a_heads = GROUP_W // A_VDIM
        rows_p.append((ka.reshape(B, T, n_a_heads, A_VDIM), va.reshape(B, T, n_a_heads, A_VDIM),
                       kbt.reshape(B, n_b_heads, HEAD_DIM, T).transpose(0, 3, 1, 2),
                       vbt.reshape(B, n_b_heads, HEAD_DIM, T).transpose(0, 3, 1, 2),
                       lft.transpose(0, 2, 1)))

        qa_s, ka_s, va_s, qb_s, kb_s, vb_s, lft_s = _proj_call(ys, g0, w_bf, bfp, cos_s, sin_s, tm=tm_s,
                                                               prompt=False)
        qa5 = qa_s.reshape(Bd, Tn, n_a_heads, 2, HEAD_DIM).transpose(0, 2, 3, 1, 4)
        qbda = (qa5[:, :, :, :, None, :] * jnp.eye(2, dtype=BF16)[None, None, :, None, :, None]
                ).reshape(Bd, n_a_heads * 2 * Tn, A_VDIM)
        qb4 = qb_s.reshape(Bd, Tn, n_b_heads, HEAD_DIM).transpose(0, 2, 1, 3)
        qbdb = (qb4[:, :, :, None, :] * jnp.eye(n_b_heads, dtype=BF16)[None, :, None, :, None]
                ).reshape(Bd, n_b_heads * Tn, GROUP_W)
        lfn = jnp.pad(lft_s[0].reshape(n_b_heads, Bd, Tn).transpose(1, 0, 2), ((0, 0), (0, 0), (0, LANES - Tn)))
        xk = cache_a_k[layer].reshape(n_pool, page * n_a_heads, A_VDIM)
        xv = cache_a_v[layer].reshape(n_pool, page * n_a_heads, A_VDIM)
        kt = cache_b_k[layer].transpose(0, 2, 3, 1).reshape(n_pool, n_b_heads * HEAD_DIM, page)
        vt = cache_b_v[layer].transpose(0, 2, 3, 1).reshape(n_pool, n_b_heads * HEAD_DIM, page)
        lf = cache_b_logf[layer].transpose(0, 2, 1)
        oa_s, ob_s = _sample_attn_call(pt_flat, qbda, qbdb, ka_s[0], va_s[0], kb_s[0], vb_s[0], lfn, lamv, gs,
                                       xk, xv, kt, vt, lf, n_pages=n_pages, n_pp=n_pp, lam_init=lam_init)
        ys = _post_call(oa_s, ob_s, ys[0], *post_w, tm=tm_s, ff_chunk=ff_chunk).reshape(1, n_tok_s, D)
        rows_s.append((ka_s.reshape(Bd, Tn, n_a_heads, A_VDIM), va_s.reshape(Bd, Tn, n_a_heads, A_VDIM),
                       kb_s.reshape(Bd, Tn, n_b_heads, HEAD_DIM), vb_s.reshape(Bd, Tn, n_b_heads, HEAD_DIM),
                       lft_s[0].reshape(n_b_heads, Bd, Tn).transpose(1, 2, 0)))

    stack = lambda rows, i: jnp.stack([r[i] for r in rows], axis=0)
    return (yp, ys.reshape(Bd, Tn, D),
            stack(rows_p, 0), stack(rows_p, 1), stack(rows_p, 2), stack(rows_p, 3), stack(rows_p, 4),
            stack(rows_s, 0), stack(rows_s, 1), stack(rows_s, 2), stack(rows_s, 3), stack(rows_s, 4))
```

```python
import functools
import math

import jax
import jax.numpy as jnp
from jax import lax
from jax.experimental import pallas as pl
from jax.experimental.pallas import tpu as pltpu

F32 = jnp.float32
BF16 = jnp.bfloat16

HEAD_DIM = 64
A_VDIM = 2 * HEAD_DIM
ROPE_THETA = 10000.0
RMS_EPS = 1e-6
NEG_INF = -1e30
ATTN_SCALE = HEAD_DIM ** -0.5
LOG2E = math.log2(math.e)

LANES = 128
SUBLANES = 8
GROUP_W = 512
VMEM_LIMIT = 56 * 1024 * 1024


def _lambda_init(layer):
    return 0.8 - 0.6 * math.exp(-0.3 * layer)


def _rms(x, g):
    return x * lax.rsqrt(jnp.mean(x * x, axis=-1, keepdims=True) + RMS_EPS) * g


def _lam_value(lamv_ref, lam_init):
    v = lamv_ref[...]
    a = jnp.sum(v[0:1] * v[1:2], axis=1, keepdims=True)
    b = jnp.sum(v[2:3] * v[3:4], axis=1, keepdims=True)
    return jnp.exp(a) - jnp.exp(b) + lam_init


def _cumsum_lanes(x):
    lane = lax.broadcasted_iota(jnp.int32, x.shape, 1)
    for s in (1, 2, 4, 8, 16, 32, 64):
        x = x + jnp.where(lane >= s, pltpu.roll(x, s, 1), 0.0)
    return x


def _proj_kernel(x_ref, g_ref, w_ref, bf_ref, cos_ref, sin_ref, *outs, prompt):
    if prompt:
        qa_ref, ka_ref, va_ref, kab_ref, vab_ref, qb_ref, kb_ref, vb_ref, kbb_ref, vbb_ref, lft_ref = outs
    else:
        qa_ref, ka_ref, va_ref, qb_ref, kb_ref, vb_ref, lft_ref = outs
    n_a_heads = GROUP_W // A_VDIM
    x = x_ref[...]
    tm = x.shape[0]
    hb = _rms(x, g_ref[...]).astype(BF16)
    cos4 = jnp.concatenate([cos_ref[...]] * 4, axis=1)
    sin4 = jnp.concatenate([sin_ref[...]] * 4, axis=1)
    lane = lax.broadcasted_iota(jnp.int32, (tm, GROUP_W), 1)
    first_half = (lane % HEAD_DIM) < (HEAD_DIM // 2)

    def col(j, width=GROUP_W):
        return jnp.dot(hb, w_ref[:, j * GROUP_W:j * GROUP_W + width], preferred_element_type=F32)

    def rope(y):
        partner = jnp.where(first_half, pltpu.roll(y, GROUP_W - HEAD_DIM // 2, 1),
                            pltpu.roll(y, HEAD_DIM // 2, 1))
        return y * cos4 + partner * sin4

    def store_rows_by_head(ref, y):
        for h in range(n_a_heads):
            ref[pl.ds(h, tm, stride=n_a_heads), :] = y[:, h * A_VDIM:(h + 1) * A_VDIM]

    q_scale = ATTN_SCALE * LOG2E if prompt else ATTN_SCALE
    qa_ref[...] = (rope(col(0)) * q_scale).astype(BF16)
    ka = rope(col(1))
    va = col(2)
    store_rows_by_head(ka_ref, ka)
    store_rows_by_head(va_ref, va)
    qb_ref[...] = (col(3) * q_scale).astype(BF16)
    kb = col(4)
    vb = col(5)
    if prompt:
        kab_ref[...] = ka.astype(BF16)
        vab_ref[...] = va.astype(BF16)
        kbt = kb.T
        vbt = vb.T
        kb_ref[...] = kbt
        vb_ref[...] = vbt
        kbb_ref[...] = kbt.astype(BF16)
        vbb_ref[...] = vbt.astype(BF16)
    else:
        kb_ref[...] = kb
        vb_ref[...] = vb
    fl = col(6, LANES) + bf_ref[...]
    lf = jnp.minimum(fl, 0.0) - jnp.log1p(jnp.exp(-jnp.abs(fl)))
    lft_ref[...] = lf.T[:SUBLANES, :]


def _proj_call(x3, g, w_bf, bfp, cos_t, sin_t, *, tm, prompt):
    B, T, D = x3.shape
    n_a_heads = GROUP_W // A_VDIM
    nt = T // tm
    ntab = cos_t.shape[0] // tm
    row = lambda b, i: (b, i, 0)
    colm = lambda b, i: (b, 0, i)
    const = lambda b, i: (0, 0)
    tab = lambda b, i: (i % ntab, 0)
    tok_spec = pl.BlockSpec((None, tm, GROUP_W), row)
    tok = lambda dt: jax.ShapeDtypeStruct((B, T, GROUP_W), dt)
    il_spec = pl.BlockSpec((None, tm * n_a_heads, A_VDIM), row)
    il = jax.ShapeDtypeStruct((B, T * n_a_heads, A_VDIM), F32)
    tr_spec = pl.BlockSpec((None, GROUP_W, tm), colm)
    tr = lambda dt: jax.ShapeDtypeStruct((B, GROUP_W, T), dt)
    lft_spec = pl.BlockSpec((None, SUBLANES, tm), colm)
    lft = jax.ShapeDtypeStruct((B, SUBLANES, T), F32)
    if prompt:
        out_specs = [tok_spec, il_spec, il_spec, tok_spec, tok_spec, tok_spec, tr_spec, tr_spec, tr_spec, tr_spec,
                     lft_spec]
        out_shape = [tok(BF16), il, il, tok(BF16), tok(BF16), tok(BF16), tr(F32), tr(F32), tr(BF16), tr(BF16), lft]
    else:
        out_specs = [tok_spec, il_spec, il_spec, tok_spec, tok_spec, tok_spec, lft_spec]
        out_shape = [tok(BF16), il, il, tok(BF16), tok(F32), tok(F32), lft]
    return pl.pallas_call(
        functools.partial(_proj_kernel, prompt=prompt),
        grid=(B, nt),
        in_specs=[
            pl.BlockSpec((None, tm, D), row),
            pl.BlockSpec((1, D), const),
            pl.BlockSpec(w_bf.shape, const),
            pl.BlockSpec((1, LANES), const),
            pl.BlockSpec((tm, LANES), tab),
            pl.BlockSpec((tm, LANES), tab),
        ],
        out_specs=out_specs,
        out_shape=out_shape,
        compiler_params=pltpu.CompilerParams(
            dimension_semantics=("arbitrary", "arbitrary"), vmem_limit_bytes=VMEM_LIMIT),
        name="proj",
    )(x3, g, w_bf, bfp, cos_t, sin_t)


_NT = (((1,), (1,)), ((), ()))


def _flash_groups(groups, m_sc, acc_sc, *, seq, tq):
    nq = seq // tq
    half = tq // 2
    lane = lax.broadcasted_iota(jnp.int32, (tq, LANES), 1)

    def qblock(i, carry):
        q0 = pl.multiple_of(i * tq, tq)
        qs = []
        for q_ref, *_ in groups:
            q = q_ref[pl.ds(q0, tq), :]
            zero = jnp.zeros_like(q)
            qs.append((jnp.where(lane < HEAD_DIM, q, zero), jnp.where(lane >= HEAD_DIM, q, zero)))
        m_sc[...] = jnp.full(m_sc.shape, NEG_INF, F32)
        acc_sc[...] = jnp.zeros(acc_sc.shape, F32)

        def block(r0, nr, k0, kw, masked):
            if masked:
                rowg = q0 + r0 + lax.broadcasted_iota(jnp.int32, (nr, kw), 0)
                colg = k0 + lax.broadcasted_iota(jnp.int32, (nr, kw), 1)
                keep = colg <= rowg
            for g, (_, k_ref, vext_sc, bias_row, _, k_transposed) in enumerate(groups):
                if k_transposed:
                    k = k_ref[:, pl.ds(k0, kw)]
                    v = vext_sc[:, pl.ds(k0, kw)]
                else:
                    k = k_ref[pl.ds(k0, kw), :]
                    v = vext_sc[pl.ds(k0, kw), :]
                for c in range(2):
                    n = 2 * g + c
                    qc = qs[g][c][r0:r0 + nr]
                    if k_transposed:
                        s = jnp.dot(qc, k, preferred_element_type=F32)
                    else:
                        s = lax.dot_general(qc, k, _NT, preferred_element_type=F32)
                    if bias_row is not None:
                        s = s - bias_row(c, k0, kw)
                    if masked:
                        s = jnp.where(keep, s, NEG_INF)
                    m_prev = m_sc[n, r0:r0 + nr, :]
                    m_new = jnp.maximum(m_prev, jnp.max(s, axis=1, keepdims=True))
                    corr = jnp.exp2(m_prev - m_new)
                    p = jnp.exp2(s - jnp.concatenate([m_new] * (kw // LANES), axis=1))
                    pb = p.astype(BF16)
                    if k_transposed:
                        pv = lax.dot_general(pb, v, _NT, preferred_element_type=F32)
                    else:
                        pv = jnp.dot(pb, v, preferred_element_type=F32)
                    acc_sc[n, r0:r0 + nr, :] = acc_sc[n, r0:r0 + nr, :] * jnp.concatenate([corr, corr], axis=1) + pv
                    m_sc[n, r0:r0 + nr, :] = m_new

        def full_block(j, c):
            block(0, tq, pl.multiple_of(j * tq, tq), tq, False)
            return c

        lax.fori_loop(0, i, full_block, 0)
        block(0, tq, q0, half, True)
        block(half, half, pl.multiple_of(q0 + half, half), half, True)
        for g, grp in enumerate(groups):
            acc0 = acc_sc[2 * g]
            acc1 = acc_sc[2 * g + 1]
            grp[4](q0, acc0[:, :LANES] / acc0[:, LANES:], acc1[:, :LANES] / acc1[:, LANES:])
        return carry

    lax.fori_loop(0, nq, qblock, 0)


def _prompt_attn_kernel(qa_ref, ka_ref, va_ref, qb_ref, kbt_ref, vbt_ref, lft_ref, lamv_ref, gs_ref,
                        oa_ref, ob_ref,
                        vaext_sc, vbext_sc, cum_sc, m_sc, acc_sc, *, tq, lam_init):
    seq = qa_ref.shape[0]
    j = pl.program_id(1)

    @pl.when(j == 0)
    def _():
        carry = jnp.zeros((SUBLANES, LANES), F32)
        for c in range(seq // LANES):
            x = lft_ref[:, c * LANES:(c + 1) * LANES]
            cum_sc[:, c * LANES:(c + 1) * LANES] = (_cumsum_lanes(x) + carry) * LOG2E
            carry = carry + jnp.sum(x, axis=1, keepdims=True)

    vaext_sc[:, :LANES] = va_ref[...]
    vaext_sc[:, LANES:] = jnp.ones((seq, LANES), BF16)
    vbext_sc[:LANES, :] = vbt_ref[...]
    vbext_sc[LANES:, :] = jnp.ones((LANES, seq), BF16)

    lam = _lam_value(lamv_ref, lam_init)
    gs = gs_ref[...]
    lane = lax.broadcasted_iota(jnp.int32, (tq, LANES), 1)

    def write_a(q0, o1, o2):
        o = o1 - lam * o2
        oa_ref[pl.ds(q0, tq), :] = (_rms(o, gs) * (1.0 - lam_init)).astype(BF16)

    def write_b(q0, o1, o2):
        ob_ref[pl.ds(q0, tq), :] = jnp.where(lane < HEAD_DIM, o1, o2).astype(BF16)

    def bias_b(c, k0, kw):
        return cum_sc[pl.ds(2 * j + c, 1), pl.ds(k0, kw)]

    _flash_groups([(qa_ref, ka_ref, vaext_sc, None, write_a, False),
                   (qb_ref, kbt_ref, vbext_sc, bias_b, write_b, True)], m_sc, acc_sc, seq=seq, tq=tq)


def _prompt_attn_call(qa, ka, va, qb, kbt, vbt, lft, lamv, gs, *, lam_init, tq):
    B, T, _ = qa.shape
    n_pair = GROUP_W // LANES
    tokc = pl.BlockSpec((None, T, LANES), lambda b, j: (b, 0, j))
    tr = pl.BlockSpec((None, LANES, T), lambda b, j: (b, j, 0))
    const2 = lambda b, j: (0, 0)
    return pl.pallas_call(
        functools.partial(_prompt_attn_kernel, tq=tq, lam_init=lam_init),
        grid=(B, n_pair),
        in_specs=[tokc, tokc, tokc, tokc, tr, tr,
                  pl.BlockSpec((None, SUBLANES, T), lambda b, j: (b, 0, 0)),
                  pl.BlockSpec((4, HEAD_DIM), const2),
                  pl.BlockSpec((1, A_VDIM), const2)],
        out_specs=[tokc, tokc],
        out_shape=[jax.ShapeDtypeStruct((B, T, GROUP_W), BF16), jax.ShapeDtypeStruct((B, T, GROUP_W), BF16)],
        scratch_shapes=[
            pltpu.VMEM((T, 2 * LANES), BF16), pltpu.VMEM((2 * LANES, T), BF16),
            pltpu.VMEM((SUBLANES, T), F32),
            pltpu.VMEM((4, tq, LANES), F32), pltpu.VMEM((4, tq, 2 * LANES), F32),
        ],
        compiler_params=pltpu.CompilerParams(
            dimension_semantics=("arbitrary", "arbitrary"), vmem_limit_bytes=VMEM_LIMIT),
        name="prompt_attn",
    )(qa, ka, va, qb, kbt, vbt, lft, lamv, gs)


def _sample_attn_kernel(pt_ref, qbda_ref, qbdb_ref, kan_ref, van_ref, kbn_ref, vbn_ref, lfn_ref, lamv_ref, gs_ref,
                        *rest, n_pp, n_a_heads, lam_init):
    pages = rest[:5 * n_pp]
    oa_ref, ob_ref = rest[5 * n_pp:5 * n_pp + 2]
    m_sc, l_sc, acca_sc, accbt_sc, carry_sc = rest[5 * n_pp + 2:]
    xk = pages[0 * n_pp:1 * n_pp]
    xv = pages[1 * n_pp:2 * n_pp]
    kt = pages[2 * n_pp:3 * n_pp]
    vt = pages[3 * n_pp:4 * n_pp]
    lf = pages[4 * n_pp:5 * n_pp]
    g = pl.program_id(1)
    n_rows = 2 * SUBLANES * n_a_heads
    tn = SUBLANES

    @pl.when(g == 0)
    def _():
        m_sc[...] = jnp.full(m_sc.shape, NEG_INF, F32)
        l_sc[...] = jnp.zeros(l_sc.shape, F32)
        acca_sc[...] = jnp.zeros(acca_sc.shape, F32)
        accbt_sc[...] = jnp.zeros(accbt_sc.shape, F32)
        carry_sc[...] = jnp.zeros(carry_sc.shape, F32)

    qbda = qbda_ref[...]
    qbdb = qbdb_ref[...]
    eye = (lax.broadcasted_iota(jnp.int32, (LANES, LANES), 0)
           == lax.broadcasted_iota(jnp.int32, (LANES, LANES), 1)).astype(F32)

    def attend(k_heads, v_heads, ktile, vtile, ck, keep):
        w = ck.shape[1]
        s_rows = [lax.dot_general(qbda[h * 2 * tn:(h + 1) * 2 * tn], k_heads[h], _NT, preferred_element_type=F32)
                  for h in range(n_a_heads)]
        bias = jnp.broadcast_to(ck[:, None, :], (SUBLANES, tn, w)).reshape(SUBLANES * tn, w)
        s_rows.append(jnp.dot(qbdb, ktile, preferred_element_type=F32) - bias)
        s = jnp.concatenate(s_rows, axis=0)
        if keep is not None:
            s = jnp.where(keep, s, NEG_INF)
        m_prev = m_sc[...]
        m_new = jnp.maximum(m_prev, jnp.max(s, axis=1, keepdims=True))
        corr = jnp.exp(m_prev - m_new)
        p = jnp.exp(s - jnp.concatenate([m_new] * (w // LANES), axis=1))
        l_sc[...] = corr * l_sc[...] + jnp.sum(p, axis=1, keepdims=True)
        m_sc[...] = m_new
        pb = p.astype(BF16)
        for h in range(n_a_heads):
            r = slice(h * 2 * tn, (h + 1) * 2 * tn)
            acca_sc[r, :] = acca_sc[r, :] * corr[r] + jnp.dot(pb[r], v_heads[h], preferred_element_type=F32)
        corr_row = jnp.sum(eye * corr, axis=0, keepdims=True)
        accbt_sc[...] = accbt_sc[...] * corr_row + lax.dot_general(vtile, pb, _NT, preferred_element_type=F32)

    carry = carry_sc[...]
    cks = []
    for j in range(n_pp):
        x = lf[j][...]
        cks.append(_cumsum_lanes(x) + carry)
        carry = carry + jnp.sum(x, axis=1, keepdims=True)
    page = xk[0].shape[0] // n_a_heads

    def head_rows(refs, h):
        return jnp.concatenate([r[pl.ds(h, page, stride=n_a_heads), :] for r in refs], axis=0).astype(BF16)

    attend([head_rows(xk, h) for h in range(n_a_heads)],
           [head_rows(xv, h) for h in range(n_a_heads)],
           jnp.concatenate([r[...] for r in kt], axis=1).astype(BF16),
           jnp.concatenate([r[...] for r in vt], axis=1).astype(BF16),
           jnp.concatenate(cks, axis=1), None)
    carry_sc[...] = carry

    @pl.when(g == pl.num_programs(1) - 1)
    def _():
        pad = lambda a: jnp.concatenate([a, jnp.zeros((LANES - tn, a.shape[1]), a.dtype)], axis=0)
        new_head = lambda ref, h: pad(ref[pl.ds(h, tn, stride=n_a_heads), :]).astype(BF16)
        ck_new = _cumsum_lanes(lfn_ref[...]) + carry
        rowq = lax.broadcasted_iota(jnp.int32, (LANES, LANES), 0) % tn
        key = lax.broadcasted_iota(jnp.int32, (LANES, LANES), 1)
        attend([new_head(kan_ref, h) for h in range(n_a_heads)],
               [new_head(van_ref, h) for h in range(n_a_heads)],
               pad(kbn_ref[...]).T.astype(BF16), pad(vbn_ref[...]).T.astype(BF16),
               ck_new, key <= rowq)

        lam = _lam_value(lamv_ref, lam_init)
        gs = gs_ref[...]
        inv_l = 1.0 / l_sc[...]
        for h in range(n_a_heads):
            r1 = slice(h * 2 * tn, h * 2 * tn + tn)
            r2 = slice(h * 2 * tn + tn, (h + 1) * 2 * tn)
            o = acca_sc[r1, :] * inv_l[r1] - lam * (acca_sc[r2, :] * inv_l[r2])
            oa_ref[:, h * A_VDIM:(h + 1) * A_VDIM] = (_rms(o, gs) * (1.0 - lam_init)).astype(BF16)
        inv_lb = jnp.concatenate([inv_l[n_rows:]] * (GROUP_W // LANES), axis=1)
        ob_full = accbt_sc[...].T[n_rows:, :] * inv_lb
        lane = lax.broadcasted_iota(jnp.int32, (tn, LANES), 1)
        for jb in range(GROUP_W // LANES):
            lo = ob_full[(2 * jb) * tn:(2 * jb + 1) * tn, jb * LANES:(jb + 1) * LANES]
            hi = ob_full[(2 * jb + 1) * tn:(2 * jb + 2) * tn, jb * LANES:(jb + 1) * LANES]
            ob_ref[:, jb * LANES:(jb + 1) * LANES] = jnp.where(lane < HEAD_DIM, lo, hi).astype(BF16)


def _sample_attn_call(pt_flat, qbda, qbdb, kan, van, kbn, vbn, lfn, lamv, gs, xk, xv, kt, vt, lf,
                      *, n_pages, n_pp, lam_init):
    Bd = qbda.shape[0]
    n_a_heads = GROUP_W // A_VDIM
    steps = n_pages // n_pp
    per_b = lambda b, g, pt: (b, 0, 0)
    tok = pl.BlockSpec((SUBLANES, GROUP_W), lambda b, g, pt: (b, 0))
    tok_il = pl.BlockSpec((SUBLANES * n_a_heads, A_VDIM), lambda b, g, pt: (b, 0))
    const2 = lambda b, g, pt: (0, 0)

    def page_spec(rows, j):
        return pl.BlockSpec((None, rows, LANES), lambda b, g, pt: (pt[b * n_pages + g * n_pp + j], 0, 0))

    page_specs, page_args = [], []
    for arr in (xk, xv, kt, vt, lf):
        for j in range(n_pp):
            page_specs.append(page_spec(arr.shape[1], j))
            page_args.append(arr)
    grid_spec = pltpu.PrefetchScalarGridSpec(
        num_scalar_prefetch=1,
        grid=(Bd, steps),
        in_specs=[pl.BlockSpec((None, 2 * SUBLANES * n_a_heads, LANES), per_b),
                  pl.BlockSpec((None, SUBLANES * SUBLANES, GROUP_W), per_b),
                  tok_il, tok_il, tok, tok,
                  pl.BlockSpec((None, SUBLANES, LANES), per_b),
                  pl.BlockSpec((4, HEAD_DIM), const2),
                  pl.BlockSpec((1, A_VDIM), const2)] + page_specs,
        out_specs=[tok, tok],
        scratch_shapes=[pltpu.VMEM((LANES, LANES), F32), pltpu.VMEM((LANES, LANES), F32),
                        pltpu.VMEM((2 * SUBLANES * n_a_heads, LANES), F32),
                        pltpu.VMEM((GROUP_W, LANES), F32),
                        pltpu.VMEM((SUBLANES, LANES), F32)],
    )
    n_tok = Bd * SUBLANES
    return pl.pallas_call(
        functools.partial(_sample_attn_kernel, n_pp=n_pp, n_a_heads=n_a_heads, lam_init=lam_init),
        grid_spec=grid_spec,
        out_shape=[jax.ShapeDtypeStruct((n_tok, GROUP_W), BF16), jax.ShapeDtypeStruct((n_tok, GROUP_W), BF16)],
        compiler_params=pltpu.CompilerParams(
            dimension_semantics=("arbitrary", "arbitrary"), vmem_limit_bytes=VMEM_LIMIT),
        name="sample_attn",
    )(pt_flat, qbda, qbdb, kan, van, kbn, vbn, lfn, lamv, gs, *page_args)


def _post_kernel(oa_ref, ob_ref, x_ref, wo_ref, g1_ref, g2_ref, wg_ref, wu_ref, wd_ref, g3_ref, y_ref, *, ff_chunk):
    wa = oa_ref.shape[1]
    a = (jnp.dot(oa_ref[...], wo_ref[:wa, :], preferred_element_type=F32)
         + jnp.dot(ob_ref[...], wo_ref[wa:, :], preferred_element_type=F32))
    x1 = x_ref[...] + _rms(a, g1_ref[...])
    hb = _rms(x1, g2_ref[...]).astype(BF16)
    d_ff = wg_ref.shape[1]
    d = None
    for c in range(d_ff // ff_chunk):
        cs = slice(c * ff_chunk, (c + 1) * ff_chunk)
        gate = jnp.dot(hb, wg_ref[:, cs], preferred_element_type=F32)
        up = jnp.dot(hb, wu_ref[:, cs], preferred_element_type=F32)
        f = (gate * jax.nn.sigmoid(gate) * up).astype(BF16)
        part = jnp.dot(f, wd_ref[cs, :], preferred_element_type=F32)
        d = part if d is None else d + part
    y_ref[...] = x1 + _rms(d, g3_ref[...])


def _post_call(oa, ob, x2, wo, g1, g2, wg, wu, wd, g3, *, tm, ff_chunk):
    N, D = x2.shape
    row = lambda i: (i, 0)
    const = lambda i: (0, 0)
    resident = lambda a: pl.BlockSpec(a.shape, const, pipeline_mode=pl.Buffered(1))
    vec = pl.BlockSpec((1, D), const)
    return pl.pallas_call(
        functools.partial(_post_kernel, ff_chunk=ff_chunk),
        grid=(N // tm,),
        in_specs=[pl.BlockSpec((tm, GROUP_W), row), pl.BlockSpec((tm, GROUP_W), row), pl.BlockSpec((tm, D), row),
                  resident(wo), vec, vec, resident(wg), resident(wu), resident(wd), vec],
        out_specs=pl.BlockSpec((tm, D), row),
        out_shape=jax.ShapeDtypeStruct((N, D), F32),
        compiler_params=pltpu.CompilerParams(
            dimension_semantics=("arbitrary",), vmem_limit_bytes=VMEM_LIMIT),
        name="post",
    )(oa, ob, x2, wo, g1, g2, wg, wu, wd, g3)


def _rope_tables(pos):
    half = HEAD_DIM // 2
    inv = ROPE_THETA ** (-jnp.arange(half, dtype=F32) / half)
    ang = pos[:, None] * inv[None, :]
    cos = jnp.tile(jnp.cos(ang), (1, LANES // half))
    sign = jnp.where((jnp.arange(LANES) % HEAD_DIM) < half, -1.0, 1.0).astype(F32)
    sin = jnp.tile(jnp.sin(ang), (1, LANES // half)) * sign[None, :]
    return cos, sin


def _pick(n, prefs):
    for p in prefs:
        if n % p == 0:
            return p
    return n


def kernel(x_prompt, x_sample, cache_a_k, cache_a_v, cache_b_k, cache_b_v, cache_b_logf, page_table,
           g_pre_mix, w_in, b_f, lam_q1, lam_k1, lam_q2, lam_k2, g_subln, w_out, g_post_mix,
           g_pre_ffn, w_gate, w_up, w_down, g_post_ffn):
    B, T, D = x_prompt.shape
    Bd, Tn, _ = x_sample.shape
    depth = w_in.shape[0]
    n_pool, page = cache_a_k.shape[1], cache_a_k.shape[2]
    n_pages = page_table.shape[1]
    past_len = n_pages * page
    d_ff = w_gate.shape[2]
    n_b_heads = cache_b_k.shape[3]
    in_cols = w_in.shape[2]
    assert D == 2 * GROUP_W and Tn == SUBLANES and page == LANES and n_b_heads == SUBLANES
    assert in_cols == 6 * GROUP_W + n_b_heads

    tm_p = _pick(T, (512, 256, 128))
    n_tok_s = Bd * Tn
    tm_s = _pick(n_tok_s, (512, 256, 128, 8))
    tq = _pick(T, (512, 256, 128))
    n_pp = _pick(n_pages, (8, 4, 2, 1))
    ff_chunk = _pick(d_ff, (1408, 1024, 512, 256, 128))

    cos_p, sin_p = _rope_tables(jnp.arange(T, dtype=F32))
    pos_s = past_len + jnp.arange(Tn, dtype=F32)
    cos_s, sin_s = _rope_tables(jnp.tile(pos_s, tm_s // Tn))

    pt_flat = page_table.reshape(-1).astype(jnp.int32)
    yp = x_prompt
    ys = x_sample.reshape(1, n_tok_s, D)
    rows_p, rows_s = [], []
    for layer in range(depth):
        lam_init = _lambda_init(layer)
        w_bf = jnp.pad(w_in[layer].astype(BF16), ((0, 0), (0, 6 * GROUP_W + LANES - in_cols)))
        bfp = jnp.pad(b_f[layer], (0, LANES - n_b_heads)).reshape(1, LANES)
        g0 = g_pre_mix[layer].reshape(1, D)
        lamv = jnp.stack([lam_q1[layer], lam_k1[layer], lam_q2[layer], lam_k2[layer]]).astype(F32)
        gs = g_subln[layer].reshape(1, A_VDIM)
        post_w = (w_out[layer].astype(BF16), g_post_mix[layer].reshape(1, D), g_pre_ffn[layer].reshape(1, D),
                  w_gate[layer].astype(BF16), w_up[layer].astype(BF16), w_down[layer].astype(BF16),
                  g_post_ffn[layer].reshape(1, D))

        qa, ka, va, ka_bf, va_bf, qb, kbt, vbt, kbt_bf, vbt_bf, lft = _proj_call(
            yp, g0, w_bf, bfp, cos_p, sin_p, tm=tm_p, prompt=True)
        oa, ob = _prompt_attn_call(qa, ka_bf, va_bf, qb, kbt_bf, vbt_bf, lft, lamv, gs, lam_init=lam_init, tq=tq)
        yp = _post_call(oa.reshape(B * T, GROUP_W), ob.reshape(B * T, GROUP_W), yp.reshape(B * T, D), *post_w,
                        tm=tm_p, ff_chunk=ff_chunk).reshape(B, T, D)
        n_a_heads = GROUP_W // A_VDIM
        rows_p.append((ka.reshape(B, T, n_a_heads, A_VDIM), va.reshape(B, T, n_a_heads, A_VDIM),
                       kbt.reshape(B, n_b_heads, HEAD_DIM, T).transpose(0, 3, 1, 2),
                       vbt.reshape(B, n_b_heads, HEAD_DIM, T).transpose(0, 3, 1, 2),
                       lft.transpose(0, 2, 1)))

        qa_s, ka_s, va_s, qb_s, kb_s, vb_s, lft_s = _proj_call(ys, g0, w_bf, bfp, cos_s, sin_s, tm=tm_s,
                                                               prompt=False)
        qa5 = qa_s.reshape(Bd, Tn, n_a_heads, 2, HEAD_DIM).transpose(0, 2, 3, 1, 4)
        qbda = (qa5[:, :, :, :, None, :] * jnp.eye(2, dtype=BF16)[None, None, :, None, :, None]
                ).reshape(Bd, n_a_heads * 2 * Tn, A_VDIM)
        qb4 = qb_s.reshape(Bd, Tn, n_b_heads, HEAD_DIM).transpose(0, 2, 1, 3)
        qbdb = (qb4[:, :, :, None, :] * jnp.eye(n_b_heads, dtype=BF16)[None, :, None, :, None]
                ).reshape(Bd, n_b_heads * Tn, GROUP_W)
        lfn = jnp.pad(lft_s[0].reshape(n_b_heads, Bd, Tn).transpose(1, 0, 2), ((0, 0), (0, 0), (0, LANES - Tn)))
        xk = cache_a_k[layer].reshape(n_pool, page * n_a_heads, A_VDIM)
        xv = cache_a_v[layer].reshape(n_pool, page * n_a_heads, A_VDIM)
        kt = cache_b_k[layer].transpose(0, 2, 3, 1).reshape(n_pool, n_b_heads * HEAD_DIM, page)
        vt = cache_b_v[layer].transpose(0, 2, 3, 1).reshape(n_pool, n_b_heads * HEAD_DIM, page)
        lf = cache_b_logf[layer].transpose(0, 2, 1)
        oa_s, ob_s = _sample_attn_call(pt_flat, qbda, qbdb, ka_s[0], va_s[0], kb_s[0], vb_s[0], lfn, lamv, gs,
                                       xk, xv, kt, vt, lf, n_pages=n_pages, n_pp=n_pp, lam_init=lam_init)
        ys = _post_call(oa_s, ob_s, ys[0], *post_w, tm=tm_s, ff_chunk=ff_chunk).reshape(1, n_tok_s, D)
        rows_s.append((ka_s.reshape(Bd, Tn, n_a_heads, A_VDIM), va_s.reshape(Bd, Tn, n_a_heads, A_VDIM),
                       kb_s.reshape(Bd, Tn, n_b_heads, HEAD_DIM), vb_s.reshape(Bd, Tn, n_b_heads, HEAD_DIM),
                       lft_s[0].reshape(n_b_heads, Bd, Tn).transpose(1, 2, 0)))

    stack = lambda rows, i: jnp.stack([r[i] for r in rows], axis=0)
    return (yp, ys.reshape(Bd, Tn, D),
            stack(rows_p, 0), stack(rows_p, 1), stack(rows_p, 2), stack(rows_p, 3), stack(rows_p, 4),
            stack(rows_s, 0), stack(rows_s, 1), stack(rows_s, 2), stack(rows_s, 3), stack(rows_s, 4))
```

```python
import functools
import math

import jax
import jax.numpy as jnp
from jax import lax
from jax.experimental import pallas as pl
from jax.experimental.pallas import tpu as pltpu

F32 = jnp.float32
BF16 = jnp.bfloat16

HEAD_DIM = 64
A_VDIM = 2 * HEAD_DIM
ROPE_THETA = 10000.0
RMS_EPS = 1e-6
NEG_INF = -1e30
ATTN_SCALE = HEAD_DIM ** -0.5
LOG2E = math.log2(math.e)

LANES = 128
SUBLANES = 8
GROUP_W = 512
VMEM_LIMIT = 56 * 1024 * 1024


def _lambda_init(layer):
    return 0.8 - 0.6 * math.exp(-0.3 * layer)


def _rms(x, g):
    return x * lax.rsqrt(jnp.mean(x * x, axis=-1, keepdims=True) + RMS_EPS) * g


def _lam_value(lamv_ref, lam_init):
    v = lamv_ref[...]
    a = jnp.sum(v[0:1] * v[1:2], axis=1, keepdims=True)
    b = jnp.sum(v[2:3] * v[3:4], axis=1, keepdims=True)
    return jnp.exp(a) - jnp.exp(b) + lam_init


def _cumsum_lanes(x):
    lane = lax.broadcasted_iota(jnp.int32, x.shape, 1)
    for s in (1, 2, 4, 8, 16, 32, 64):
        x = x + jnp.where(lane >= s, pltpu.roll(x, s, 1), 0.0)
    return x


def _proj_kernel(x_ref, g_ref, w_ref, bf_ref, cos_ref, sin_ref, *outs, prompt):
    if prompt:
        qa_ref, ka_ref, va_ref, kab_ref, vab_ref, qb_ref, kb_ref, vb_ref, kbb_ref, vbb_ref, lft_ref = outs
    else:
        qa_ref, ka_ref, va_ref, qb_ref, kb_ref, vb_ref, lft_ref = outs
    n_a_heads = GROUP_W // A_VDIM
    x = x_ref[...]
    tm = x.shape[0]
    hb = _rms(x, g_ref[...]).astype(BF16)
    cos4 = jnp.concatenate([cos_ref[...]] * 4, axis=1)
    sin4 = jnp.concatenate([sin_ref[...]] * 4, axis=1)
    lane = lax.broadcasted_iota(jnp.int32, (tm, GROUP_W), 1)
    first_half = (lane % HEAD_DIM) < (HEAD_DIM // 2)

    def col(j, width=GROUP_W):
        return jnp.dot(hb, w_ref[:, j * GROUP_W:j * GROUP_W + width], preferred_element_type=F32)

    def rope(y):
        partner = jnp.where(first_half, pltpu.roll(y, GROUP_W - HEAD_DIM // 2, 1),
                            pltpu.roll(y, HEAD_DIM // 2, 1))
        return y * cos4 + partner * sin4

    def store_rows_by_head(ref, y):
        for h in range(n_a_heads):
            ref[pl.ds(h, tm, stride=n_a_heads), :] = y[:, h * A_VDIM:(h + 1) * A_VDIM]

    q_scale = ATTN_SCALE * LOG2E if prompt else ATTN_SCALE
    qa_ref[...] = (rope(col(0)) * q_scale).astype(BF16)
    ka = rope(col(1))
    va = col(2)
    store_rows_by_head(ka_ref, ka)
    store_rows_by_head(va_ref, va)
    qb_ref[...] = (col(3) * q_scale).astype(BF16)
    kb = col(4)
    vb = col(5)
    if prompt:
        kab_ref[...] = ka.astype(BF16)
        vab_ref[...] = va.astype(BF16)
        kbt = kb.T
        vbt = vb.T
        kb_ref[...] = kbt
        vb_ref[...] = vbt
        kbb_ref[...] = kbt.astype(BF16)
        vbb_ref[...] = vbt.astype(BF16)
    else:
        kb_ref[...] = kb
        vb_ref[...] = vb
    fl = col(6, LANES) + bf_ref[...]
    lf = jnp.minimum(fl, 0.0) - jnp.log1p(jnp.exp(-jnp.abs(fl)))
    lft_ref[...] = lf.T[:SUBLANES, :]


def _proj_call(x3, g, w_bf, bfp, cos_t, sin_t, *, tm, prompt):
    B, T, D = x3.shape
    n_a_heads = GROUP_W // A_VDIM
    nt = T // tm
    ntab = cos_t.shape[0] // tm
    row = lambda b, i: (b, i, 0)
    colm = lambda b, i: (b, 0, i)
    const = lambda b, i: (0, 0)
    tab = lambda b, i: (i % ntab, 0)
    tok_spec = pl.BlockSpec((None, tm, GROUP_W), row)
    tok = lambda dt: jax.ShapeDtypeStruct((B, T, GROUP_W), dt)
    il_spec = pl.BlockSpec((None, tm * n_a_heads, A_VDIM), row)
    il = jax.ShapeDtypeStruct((B, T * n_a_heads, A_VDIM), F32)
    tr_spec = pl.BlockSpec((None, GROUP_W, tm), colm)
    tr = lambda dt: jax.ShapeDtypeStruct((B, GROUP_W, T), dt)
    lft_spec = pl.BlockSpec((None, SUBLANES, tm), colm)
    lft = jax.ShapeDtypeStruct((B, SUBLANES, T), F32)
    if prompt:
        out_specs = [tok_spec, il_spec, il_spec, tok_spec, tok_spec, tok_spec, tr_spec, tr_spec, tr_spec, tr_spec,
                     lft_spec]
        out_shape = [tok(BF16), il, il, tok(BF16), tok(BF16), tok(BF16), tr(F32), tr(F32), tr(BF16), tr(BF16), lft]
    else:
        out_specs = [tok_spec, il_spec, il_spec, tok_spec, tok_spec, tok_spec, lft_spec]
        out_shape = [tok(BF16), il, il, tok(BF16), tok(F32), tok(F32), lft]
    return pl.pallas_call(
        functools.partial(_proj_kernel, prompt=prompt),
        grid=(B, nt),
        in_specs=[
            pl.BlockSpec((None, tm, D), row),
            pl.BlockSpec((1, D), const),
            pl.BlockSpec(w_bf.shape, const),
            pl.BlockSpec((1, LANES), const),
            pl.BlockSpec((tm, LANES), tab),
            pl.BlockSpec((tm, LANES), tab),
        ],
        out_specs=out_specs,
        out_shape=out_shape,
        compiler_params=pltpu.CompilerParams(
            dimension_semantics=("arbitrary", "arbitrary"), vmem_limit_bytes=VMEM_LIMIT),
        name="proj",
    )(x3, g, w_bf, bfp, cos_t, sin_t)


_NT = (((1,), (1,)), ((), ()))


def _flash_groups(groups, m_sc, acc_sc, *, seq, tq):
    nq = seq // tq
    half = tq // 2
    lane = lax.broadcasted_iota(jnp.int32, (tq, LANES), 1)

    def qblock(i):
        q0 = i * tq
        qs = []
        for q_ref, *_ in groups:
            q = q_ref[pl.ds(q0, tq), :]
            zero = jnp.zeros_like(q)
            qs.append((jnp.where(lane < HEAD_DIM, q, zero), jnp.where(lane >= HEAD_DIM, q, zero)))
        m_sc[...] = jnp.full(m_sc.shape, NEG_INF, F32)
        acc_sc[...] = jnp.zeros(acc_sc.shape, F32)

        def block(r0, nr, k0, kw, masked):
            if masked:
                rowg = q0 + r0 + lax.broadcasted_iota(jnp.int32, (nr, kw), 0)
                colg = k0 + lax.broadcasted_iota(jnp.int32, (nr, kw), 1)
                keep = colg <= rowg
            for g, (_, k_ref, vext_sc, bias_row, _, k_transposed) in enumerate(groups):
                if k_transposed:
                    k = k_ref[:, pl.ds(k0, kw)]
                    v = vext_sc[:, pl.ds(k0, kw)]
                else:
                    k = k_ref[pl.ds(k0, kw), :]
                    v = vext_sc[pl.ds(k0, kw), :]
                for c in range(2):
                    n = 2 * g + c
                    qc = qs[g][c][r0:r0 + nr]
                    if k_transposed:
                        s = jnp.dot(qc, k, preferred_element_type=F32)
                    else:
                        s = lax.dot_general(qc, k, _NT, preferred_element_type=F32)
                    if bias_row is not None:
                        s = s - bias_row(c, k0, kw)
                    if masked:
                        s = jnp.where(keep, s, NEG_INF)
                    m_prev = m_sc[n, r0:r0 + nr, :]
                    m_new = jnp.maximum(m_prev, jnp.max(s, axis=1, keepdims=True))
                    corr = jnp.exp2(m_prev - m_new)
                    p = jnp.exp2(s - jnp.concatenate([m_new] * (kw // LANES), axis=1))
                    pb = p.astype(BF16)
                    if k_transposed:
                        pv = lax.dot_general(pb, v, _NT, preferred_element_type=F32)
                    else:
                        pv = jnp.dot(pb, v, preferred_element_type=F32)
                    acc_sc[n, r0:r0 + nr, :] = acc_sc[n, r0:r0 + nr, :] * jnp.concatenate([corr, corr], axis=1) + pv
                    m_sc[n, r0:r0 + nr, :] = m_new

        for j in range(i):
            block(0, tq, j * tq, tq, False)
        block(0, tq, q0, half, True)
        block(half, half, q0 + half, half, True)
        for g, grp in enumerate(groups):
            acc0 = acc_sc[2 * g]
            acc1 = acc_sc[2 * g + 1]
            grp[4](q0, acc0[:, :LANES] / acc0[:, LANES:], acc1[:, :LANES] / acc1[:, LANES:])

    for i in range(nq):
        qblock(i)


def _prompt_attn_kernel(qa_ref, ka_ref, va_ref, qb_ref, kbt_ref, vbt_ref, lft_ref, lamv_ref, gs_ref,
                        oa_ref, ob_ref,
                        vaext_sc, vbext_sc, cum_sc, m_sc, acc_sc, *, tq, lam_init):
    seq = qa_ref.shape[0]
    j = pl.program_id(1)

    @pl.when(j == 0)
    def _():
        carry = jnp.zeros((SUBLANES, LANES), F32)
        for c in range(seq // LANES):
            x = lft_ref[:, c * LANES:(c + 1) * LANES]
            cum_sc[:, c * LANES:(c + 1) * LANES] = (_cumsum_lanes(x) + carry) * LOG2E
            carry = carry + jnp.sum(x, axis=1, keepdims=True)

    vaext_sc[:, :LANES] = va_ref[...]
    vaext_sc[:, LANES:] = jnp.ones((seq, LANES), BF16)
    vbext_sc[:LANES, :] = vbt_ref[...]
    vbext_sc[LANES:, :] = jnp.ones((LANES, seq), BF16)

    lam = _lam_value(lamv_ref, lam_init)
    gs = gs_ref[...]
    lane = lax.broadcasted_iota(jnp.int32, (tq, LANES), 1)

    def write_a(q0, o1, o2):
        o = o1 - lam * o2
        oa_ref[pl.ds(q0, tq), :] = (_rms(o, gs) * (1.0 - lam_init)).astype(BF16)

    def write_b(q0, o1, o2):
        ob_ref[pl.ds(q0, tq), :] = jnp.where(lane < HEAD_DIM, o1, o2).astype(BF16)

    def bias_b(c, k0, kw):
        return cum_sc[pl.ds(2 * j + c, 1), pl.ds(k0, kw)]

    _flash_groups([(qa_ref, ka_ref, vaext_sc, None, write_a, False),
                   (qb_ref, kbt_ref, vbext_sc, bias_b, write_b, True)], m_sc, acc_sc, seq=seq, tq=tq)


def _prompt_attn_call(qa, ka, va, qb, kbt, vbt, lft, lamv, gs, *, lam_init, tq):
    B, T, _ = qa.shape
    n_pair = GROUP_W // LANES
    tokc = pl.BlockSpec((None, T, LANES), lambda b, j: (b, 0, j))
    tr = pl.BlockSpec((None, LANES, T), lambda b, j: (b, j, 0))
    const2 = lambda b, j: (0, 0)
    return pl.pallas_call(
        functools.partial(_prompt_attn_kernel, tq=tq, lam_init=lam_init),
        grid=(B, n_pair),
        in_specs=[tokc, tokc, tokc, tokc, tr, tr,
                  pl.BlockSpec((None, SUBLANES, T), lambda b, j: (b, 0, 0)),
                  pl.BlockSpec((4, HEAD_DIM), const2),
                  pl.BlockSpec((1, A_VDIM), const2)],
        out_specs=[tokc, tokc],
        out_shape=[jax.ShapeDtypeStruct((B, T, GROUP_W), BF16), jax.ShapeDtypeStruct((B, T, GROUP_W), BF16)],
        scratch_shapes=[
            pltpu.VMEM((T, 2 * LANES), BF16), pltpu.VMEM((2 * LANES, T), BF16),
            pltpu.VMEM((SUBLANES, T), F32),
            pltpu.VMEM((4, tq, LANES), F32), pltpu.VMEM((4, tq, 2 * LANES), F32),
        ],
        compiler_params=pltpu.CompilerParams(
            dimension_semantics=("arbitrary", "arbitrary"), vmem_limit_bytes=VMEM_LIMIT),
        name="prompt_attn",
    )(qa, ka, va, qb, kbt, vbt, lft, lamv, gs)


def _sample_attn_kernel(pt_ref, qbda_ref, qbdb_ref, kan_ref, van_ref, kbn_ref, vbn_ref, lfn_ref, lamv_ref, gs_ref,
                        *rest, n_pp, n_a_heads, lam_init):
    pages = rest[:5 * n_pp]
    oa_ref, ob_ref = rest[5 * n_pp:5 * n_pp + 2]
    m_sc, l_sc, acca_sc, accbt_sc, carry_sc = rest[5 * n_pp + 2:]
    xk = pages[0 * n_pp:1 * n_pp]
    xv = pages[1 * n_pp:2 * n_pp]
    kt = pages[2 * n_pp:3 * n_pp]
    vt = pages[3 * n_pp:4 * n_pp]
    lf = pages[4 * n_pp:5 * n_pp]
    g = pl.program_id(1)
    n_rows = 2 * SUBLANES * n_a_heads
    tn = SUBLANES

    @pl.when(g == 0)
    def _():
        m_sc[...] = jnp.full(m_sc.shape, NEG_INF, F32)
        l_sc[...] = jnp.zeros(l_sc.shape, F32)
        acca_sc[...] = jnp.zeros(acca_sc.shape, F32)
        accbt_sc[...] = jnp.zeros(accbt_sc.shape, F32)
        carry_sc[...] = jnp.zeros(carry_sc.shape, F32)

    qbda = qbda_ref[...]
    qbdb = qbdb_ref[...]
    eye = (lax.broadcasted_iota(jnp.int32, (LANES, LANES), 0)
           == lax.broadcasted_iota(jnp.int32, (LANES, LANES), 1)).astype(F32)

    def attend(k_heads, v_heads, ktile, vtile, ck, keep):
        w = ck.shape[1]
        s_rows = [lax.dot_general(qbda[h * 2 * tn:(h + 1) * 2 * tn], k_heads[h], _NT, preferred_element_type=F32)
                  for h in range(n_a_heads)]
        bias = jnp.broadcast_to(ck[:, None, :], (SUBLANES, tn, w)).reshape(SUBLANES * tn, w)
        s_rows.append(jnp.dot(qbdb, ktile, preferred_element_type=F32) - bias)
        s = jnp.concatenate(s_rows, axis=0)
        if keep is not None:
            s = jnp.where(keep, s, NEG_INF)
        m_prev = m_sc[...]
        m_new = jnp.maximum(m_prev, jnp.max(s, axis=1, keepdims=True))
        corr = jnp.exp(m_prev - m_new)
        p = jnp.exp(s - jnp.concatenate([m_new] * (w // LANES), axis=1))
        l_sc[...] = corr * l_sc[...] + jnp.sum(p, axis=1, keepdims=True)
        m_sc[...] = m_new
        pb = p.astype(BF16)
        for h in range(n_a_heads):
            r = slice(h * 2 * tn, (h + 1) * 2 * tn)
            acca_sc[r, :] = acca_sc[r, :] * corr[r] + jnp.dot(pb[r], v_heads[h], preferred_element_type=F32)
        corr_row = jnp.sum(eye * corr, axis=0, keepdims=True)
        accbt_sc[...] = accbt_sc[...] * corr_row + lax.dot_general(vtile, pb, _NT, preferred_element_type=F32)

    carry = carry_sc[...]
    cks = []
    for j in range(n_pp):
        x = lf[j][...]
        cks.append(_cumsum_lanes(x) + carry)
        carry = carry + jnp.sum(x, axis=1, keepdims=True)
    page = xk[0].shape[0] // n_a_heads

    def head_rows(refs, h):
        return jnp.concatenate([r[pl.ds(h, page, stride=n_a_heads), :] for r in refs], axis=0).astype(BF16)

    attend([head_rows(xk, h) for h in range(n_a_heads)],
           [head_rows(xv, h) for h in range(n_a_heads)],
           jnp.concatenate([r[...] for r in kt], axis=1).astype(BF16),
           jnp.concatenate([r[...] for r in vt], axis=1).astype(BF16),
           jnp.concatenate(cks, axis=1), None)
    carry_sc[...] = carry

    @pl.when(g == pl.num_programs(1) - 1)
    def _():
        pad = lambda a: jnp.concatenate([a, jnp.zeros((LANES - tn, a.shape[1]), a.dtype)], axis=0)
        new_head = lambda ref, h: pad(ref[pl.ds(h, tn, stride=n_a_heads), :]).astype(BF16)
        ck_new = _cumsum_lanes(lfn_ref[...]) + carry
        rowq = lax.broadcasted_iota(jnp.int32, (LANES, LANES), 0) % tn
        key = lax.broadcasted_iota(jnp.int32, (LANES, LANES), 1)
        attend([new_head(kan_ref, h) for h in range(n_a_heads)],
               [new_head(van_ref, h) for h in range(n_a_heads)],
               pad(kbn_ref[...]).T.astype(BF16), pad(vbn_ref[...]).T.astype(BF16),
               ck_new, key <= rowq)

        lam = _lam_value(lamv_ref, lam_init)
        gs = gs_ref[...]
        inv_l = 1.0 / l_sc[...]
        for h in range(n_a_heads):
            r1 = slice(h * 2 * tn, h * 2 * tn + tn)
            r2 = slice(h * 2 * tn + tn, (h + 1) * 2 * tn)
            o = acca_sc[r1, :] * inv_l[r1] - lam * (acca_sc[r2, :] * inv_l[r2])
            oa_ref[:, h * A_VDIM:(h + 1) * A_VDIM] = (_rms(o, gs) * (1.0 - lam_init)).astype(BF16)
        inv_lb = jnp.concatenate([inv_l[n_rows:]] * (GROUP_W // LANES), axis=1)
        ob_full = accbt_sc[...].T[n_rows:, :] * inv_lb
        lane = lax.broadcasted_iota(jnp.int32, (tn, LANES), 1)
        for jb in range(GROUP_W // LANES):
            lo = ob_full[(2 * jb) * tn:(2 * jb + 1) * tn, jb * LANES:(jb + 1) * LANES]
            hi = ob_full[(2 * jb + 1) * tn:(2 * jb + 2) * tn, jb * LANES:(jb + 1) * LANES]
            ob_ref[:, jb * LANES:(jb + 1) * LANES] = jnp.where(lane < HEAD_DIM, lo, hi).astype(BF16)


def _sample_attn_call(pt_flat, qbda, qbdb, kan, van, kbn, vbn, lfn, lamv, gs, xk, xv, kt, vt, lf,
                      *, n_pages, n_pp, lam_init):
    Bd = qbda.shape[0]
    n_a_heads = GROUP_W // A_VDIM
    steps = n_pages // n_pp
    per_b = lambda b, g, pt: (b, 0, 0)
    tok = pl.BlockSpec((SUBLANES, GROUP_W), lambda b, g, pt: (b, 0))
    tok_il = pl.BlockSpec((SUBLANES * n_a_heads, A_VDIM), lambda b, g, pt: (b, 0))
    const2 = lambda b, g, pt: (0, 0)

    def page_spec(rows, j):
        return pl.BlockSpec((None, rows, LANES), lambda b, g, pt: (pt[b * n_pages + g * n_pp + j], 0, 0))

    page_specs, page_args = [], []
    for arr in (xk, xv, kt, vt, lf):
        for j in range(n_pp):
            page_specs.append(page_spec(arr.shape[1], j))
            page_args.append(arr)
    grid_spec = pltpu.PrefetchScalarGridSpec(
        num_scalar_prefetch=1,
        grid=(Bd, steps),
        in_specs=[pl.BlockSpec((None, 2 * SUBLANES * n_a_heads, LANES), per_b),
                  pl.BlockSpec((None, SUBLANES * SUBLANES, GROUP_W), per_b),
                  tok_il, tok_il, tok, tok,
                  pl.BlockSpec((None, SUBLANES, LANES), per_b),
                  pl.BlockSpec((4, HEAD_DIM), const2),
                  pl.BlockSpec((1, A_VDIM), const2)] + page_specs,
        out_specs=[tok, tok],
        scratch_shapes=[pltpu.VMEM((LANES, LANES), F32), pltpu.VMEM((LANES, LANES), F32),
                        pltpu.VMEM((2 * SUBLANES * n_a_heads, LANES), F32),
                        pltpu.VMEM((GROUP_W, LANES), F32),
                        pltpu.VMEM((SUBLANES, LANES), F32)],
    )
    n_tok = Bd * SUBLANES
    return pl.pallas_call(
        functools.partial(_sample_attn_kernel, n_pp=n_pp, n_a_heads=n_a_heads, lam_init=lam_init),
        grid_spec=grid_spec,
        out_shape=[jax.ShapeDtypeStruct((n_tok, GROUP_W), BF16), jax.ShapeDtypeStruct((n_tok, GROUP_W), BF16)],
        compiler_params=pltpu.CompilerParams(
            dimension_semantics=("arbitrary", "arbitrary"), vmem_limit_bytes=VMEM_LIMIT),
        name="sample_attn",
    )(pt_flat, qbda, qbdb, kan, van, kbn, vbn, lfn, lamv, gs, *page_args)


def _post_kernel(oa_ref, ob_ref, x_ref, wo_ref, g1_ref, g2_ref, wg_ref, wu_ref, wd_ref, g3_ref, y_ref, *, ff_chunk):
    wa = oa_ref.shape[1]
    a = (jnp.dot(oa_ref[...], wo_ref[:wa, :], preferred_element_type=F32)
         + jnp.dot(ob_ref[...], wo_ref[wa:, :], preferred_element_type=F32))
    x1 = x_ref[...] + _rms(a, g1_ref[...])
    hb = _rms(x1, g2_ref[...]).astype(BF16)
    d_ff = wg_ref.shape[1]
    d = None
    for c in range(d_ff // ff_chunk):
        cs = slice(c * ff_chunk, (c + 1) * ff_chunk)
        gate = jnp.dot(hb, wg_ref[:, cs], preferred_element_type=F32)
        up = jnp.dot(hb, wu_ref[:, cs], preferred_element_type=F32)
        f = (gate * jax.nn.sigmoid(gate) * up).astype(BF16)
        part = jnp.dot(f, wd_ref[cs, :], preferred_element_type=F32)
        d = part if d is None else d + part
    y_ref[...] = x1 + _rms(d, g3_ref[...])


def _post_call(oa, ob, x2, wo, g1, g2, wg, wu, wd, g3, *, tm, ff_chunk):
    N, D = x2.shape
    row = lambda i: (i, 0)
    const = lambda i: (0, 0)
    resident = lambda a: pl.BlockSpec(a.shape, const, pipeline_mode=pl.Buffered(1))
    vec = pl.BlockSpec((1, D), const)
    return pl.pallas_call(
        functools.partial(_post_kernel, ff_chunk=ff_chunk),
        grid=(N // tm,),
        in_specs=[pl.BlockSpec((tm, GROUP_W), row), pl.BlockSpec((tm, GROUP_W), row), pl.BlockSpec((tm, D), row),
                  resident(wo), vec, vec, resident(wg), resident(wu), resident(wd), vec],
        out_specs=pl.BlockSpec((tm, D), row),
        out_shape=jax.ShapeDtypeStruct((N, D), F32),
        compiler_params=pltpu.CompilerParams(
            dimension_semantics=("arbitrary",), vmem_limit_bytes=VMEM_LIMIT),
        name="post",
    )(oa, ob, x2, wo, g1, g2, wg, wu, wd, g3)


def _rope_tables(pos):
    half = HEAD_DIM // 2
    inv = ROPE_THETA ** (-jnp.arange(half, dtype=F32) / half)
    ang = pos[:, None] * inv[None, :]
    cos = jnp.tile(jnp.cos(ang), (1, LANES // half))
    sign = jnp.where((jnp.arange(LANES) % HEAD_DIM) < half, -1.0, 1.0).astype(F32)
    sin = jnp.tile(jnp.sin(ang), (1, LANES // half)) * sign[None, :]
    return cos, sin


def _pick(n, prefs):
    for p in prefs:
        if n % p == 0:
            return p
    return n


def kernel(x_prompt, x_sample, cache_a_k, cache_a_v, cache_b_k, cache_b_v, cache_b_logf, page_table,
           g_pre_mix, w_in, b_f, lam_q1, lam_k1, lam_q2, lam_k2, g_subln, w_out, g_post_mix,
           g_pre_ffn, w_gate, w_up, w_down, g_post_ffn):
    B, T, D = x_prompt.shape
    Bd, Tn, _ = x_sample.shape
    depth = w_in.shape[0]
    n_pool, page = cache_a_k.shape[1], cache_a_k.shape[2]
    n_pages = page_table.shape[1]
    past_len = n_pages * page
    d_ff = w_gate.shape[2]
    n_b_heads = cache_b_k.shape[3]
    in_cols = w_in.shape[2]
    assert D == 2 * GROUP_W and Tn == SUBLANES and page == LANES and n_b_heads == SUBLANES
    assert in_cols == 6 * GROUP_W + n_b_heads

    tm_p = _pick(T, (512, 256, 128))
    n_tok_s = Bd * Tn
    tm_s = _pick(n_tok_s, (512, 256, 128, 8))
    tq = _pick(T, (512, 256, 128))
    n_pp = _pick(n_pages, (8, 4, 2, 1))
    ff_chunk = _pick(d_ff, (1408, 1024, 512, 256, 128))

    cos_p, sin_p = _rope_tables(jnp.arange(T, dtype=F32))
    pos_s = past_len + jnp.arange(Tn, dtype=F32)
    cos_s, sin_s = _rope_tables(jnp.tile(pos_s, tm_s // Tn))

    pt_flat = page_table.reshape(-1).astype(jnp.int32)
    yp = x_prompt
    ys = x_sample.reshape(1, n_tok_s, D)
    rows_p, rows_s = [], []
    for layer in range(depth):
        lam_init = _lambda_init(layer)
        w_bf = jnp.pad(w_in[layer].astype(BF16), ((0, 0), (0, 6 * GROUP_W + LANES - in_cols)))
        bfp = jnp.pad(b_f[layer], (0, LANES - n_b_heads)).reshape(1, LANES)
        g0 = g_pre_mix[layer].reshape(1, D)
        lamv = jnp.stack([lam_q1[layer], lam_k1[layer], lam_q2[layer], lam_k2[layer]]).astype(F32)
        gs = g_subln[layer].reshape(1, A_VDIM)
        post_w = (w_out[layer].astype(BF16), g_post_mix[layer].reshape(1, D), g_pre_ffn[layer].reshape(1, D),
                  w_gate[layer].astype(BF16), w_up[layer].astype(BF16), w_down[layer].astype(BF16),
                  g_post_ffn[layer].reshape(1, D))

        qa, ka, va, ka_bf, va_bf, qb, kbt, vbt, kbt_bf, vbt_bf, lft = _proj_call(
            yp, g0, w_bf, bfp, cos_p, sin_p, tm=tm_p, prompt=True)
        oa, ob = _prompt_attn_call(qa, ka_bf, va_bf, qb, kbt_bf, vbt_bf, lft, lamv, gs, lam_init=lam_init, tq=tq)
        yp = _post_call(oa.reshape(B * T, GROUP_W), ob.reshape(B * T, GROUP_W), yp.reshape(B * T, D), *post_w,
                        tm=tm_p, ff_chunk=ff_chunk).reshape(B, T, D)
        n_a_heads = GROUP_W // A_VDIM
        rows_p.append((ka.reshape(B, T, n_a_heads, A_VDIM), va.reshape(B, T, n_a_heads, A_VDIM),
                       kbt.reshape(B, n_b_heads, HEAD_DIM, T).transpose(0, 3, 1, 2),
                       vbt.reshape(B, n_b_heads, HEAD_DIM, T).transpose(0, 3, 1, 2),
                       lft.transpose(0, 2, 1)))

        qa_s, ka_s, va_s, qb_s, kb_s, vb_s, lft_s = _proj_call(ys, g0, w_bf, bfp, cos_s, sin_s, tm=tm_s,
                                                               prompt=False)
        qa5 = qa_s.reshape(Bd, Tn, n_a_heads, 2, HEAD_DIM).transpose(0, 2, 3, 1, 4)
        qbda = (qa5[:, :, :, :, None, :] * jnp.eye(2, dtype=BF16)[None, None, :, None, :, None]
                ).reshape(Bd, n_a_heads * 2 * Tn, A_VDIM)
        qb4 = qb_s.reshape(Bd, Tn, n_b_heads, HEAD_DIM).transpose(0, 2, 1, 3)
        qbdb = (qb4[:, :, :, None, :] * jnp.eye(n_b_heads, dtype=BF16)[None, :, None, :, None]
                ).reshape(Bd, n_b_heads * Tn, GROUP_W)
        lfn = jnp.pad(lft_s[0].reshape(n_b_heads, Bd, Tn).transpose(1, 0, 2), ((0, 0), (0, 0), (0, LANES - Tn)))
        xk = cache_a_k[layer].reshape(n_pool, page * n_a_heads, A_VDIM)
        xv = cache_a_v[layer].reshape(n_pool, page * n_a_heads, A_VDIM)
        kt = cache_b_k[layer].transpose(0, 2, 3, 1).reshape(n_pool, n_b_heads * HEAD_DIM, page)
        vt = cache_b_v[layer].transpose(0, 2, 3, 1).reshape(n_pool, n_b_heads * HEAD_DIM, page)
        lf = cache_b_logf[layer].transpose(0, 2, 1)
        oa_s, ob_s = _sample_attn_call(pt_flat, qbda, qbdb, ka_s[0], va_s[0], kb_s[0], vb_s[0], lfn, lamv, gs,
                                       xk, xv, kt, vt, lf, n_pages=n_pages, n_pp=n_pp, lam_init=lam_init)
        ys = _post_call(oa_s, ob_s, ys[0], *post_w, tm=tm_s, ff_chunk=ff_chunk).reshape(1, n_tok_s, D)
        rows_s.append((ka_s.reshape(Bd, Tn, n_a_heads, A_VDIM), va_s.reshape(Bd, Tn, n_a_heads, A_VDIM),
                       kb_s.reshape(Bd, Tn, n_b_heads, HEAD_DIM), vb_s.reshape(Bd, Tn, n_b_heads, HEAD_DIM),
                       lft_s[0].reshape(n_b_heads, Bd, Tn).transpose(1, 2, 0)))

    stack = lambda rows, i: jnp.stack([r[i] for r in rows], axis=0)
    return (yp, ys.reshape(Bd, Tn, D),
            stack(rows_p, 0), stack(rows_p, 1), stack(rows_p, 2), stack(rows_p, 3), stack(rows_p, 4),
            stack(rows_s, 0), stack(rows_s, 1), stack(rows_s, 2), stack(rows_s, 3), stack(rows_s, 4))
```

```python
import functools
import math

import jax
import jax.numpy as jnp
from jax import lax
from jax.experimental import pallas as pl
from jax.experimental.pallas import tpu as pltpu

F32 = jnp.float32
BF16 = jnp.bfloat16

HEAD_DIM = 64
A_VDIM = 2 * HEAD_DIM
ROPE_THETA = 10000.0
RMS_EPS = 1e-6
NEG_INF = -1e30
ATTN_SCALE = HEAD_DIM ** -0.5
LOG2E = math.log2(math.e)

LANES = 128
SUBLANES = 8
MXU_WIDTH = 256
GROUP_W = 512
VMEM_LIMIT = 56 * 1024 * 1024


def _lambda_init(layer):
    return 0.8 - 0.6 * math.exp(-0.3 * layer)


def _rms(x, g):
    return x * lax.rsqrt(jnp.mean(x * x, axis=-1, keepdims=True) + RMS_EPS) * g


def _lam_value(lamv_ref, lam_init):
    v = lamv_ref[...]
    a = jnp.sum(v[0:1] * v[1:2], axis=1, keepdims=True)
    b = jnp.sum(v[2:3] * v[3:4], axis=1, keepdims=True)
    return jnp.exp(a) - jnp.exp(b) + lam_init


def _cumsum_lanes(x):
    lane = lax.broadcasted_iota(jnp.int32, x.shape, 1)
    for s in (1, 2, 4, 8, 16, 32, 64):
        x = x + jnp.where(lane >= s, pltpu.roll(x, s, 1), 0.0)
    return x


def _proj_kernel(x_ref, g_ref, w_ref, bf_ref, cos_ref, sin_ref, *outs, prompt):
    if prompt:
        qa_ref, ka_ref, va_ref, kab_ref, vab_ref, qb_ref, kb_ref, vb_ref, kbb_ref, vbb_ref, lft_ref = outs
    else:
        qa_ref, ka_ref, va_ref, qb_ref, kb_ref, vb_ref, lft_ref = outs
    n_a_heads = GROUP_W // A_VDIM
    x = x_ref[...]
    tm = x.shape[0]
    hb = _rms(x, g_ref[...]).astype(BF16)
    cos4 = jnp.concatenate([cos_ref[...]] * 4, axis=1)
    sin4 = jnp.concatenate([sin_ref[...]] * 4, axis=1)
    lane = lax.broadcasted_iota(jnp.int32, (tm, GROUP_W), 1)
    first_half = (lane % HEAD_DIM) < (HEAD_DIM // 2)

    def col(j, width=GROUP_W):
        return jnp.dot(hb, w_ref[:, j * GROUP_W:j * GROUP_W + width], preferred_element_type=F32)

    def rope(y):
        partner = jnp.where(first_half, pltpu.roll(y, GROUP_W - HEAD_DIM // 2, 1),
                            pltpu.roll(y, HEAD_DIM // 2, 1))
        return y * cos4 + partner * sin4

    def store_rows_by_head(ref, y):
        for h in range(n_a_heads):
            ref[pl.ds(h, tm, stride=n_a_heads), :] = y[:, h * A_VDIM:(h + 1) * A_VDIM]

    q_scale = ATTN_SCALE * LOG2E if prompt else ATTN_SCALE
    qa_ref[...] = (rope(col(0)) * q_scale).astype(BF16)
    ka = rope(col(1))
    va = col(2)
    store_rows_by_head(ka_ref, ka)
    store_rows_by_head(va_ref, va)
    qb_ref[...] = (col(3) * q_scale).astype(BF16)
    kb = col(4)
    vb = col(5)
    if prompt:
        kab_ref[...] = ka.astype(BF16)
        vab_ref[...] = va.astype(BF16)
        kbt = kb.T
        vbt = vb.T
        kb_ref[...] = kbt
        vb_ref[...] = vbt
        kbb_ref[...] = kbt.astype(BF16)
        vbb_ref[...] = vbt.astype(BF16)
    else:
        kb_ref[...] = kb
        vb_ref[...] = vb
    fl = col(6, LANES) + bf_ref[...]
    lf = jnp.minimum(fl, 0.0) - jnp.log1p(jnp.exp(-jnp.abs(fl)))
    lft_ref[...] = lf.T[:SUBLANES, :]


def _proj_call(x3, g, w_bf, bfp, cos_t, sin_t, *, tm, prompt):
    B, T, D = x3.shape
    n_a_heads = GROUP_W // A_VDIM
    nt = T // tm
    ntab = cos_t.shape[0] // tm
    row = lambda b, i: (b, i, 0)
    colm = lambda b, i: (b, 0, i)
    const = lambda b, i: (0, 0)
    tab = lambda b, i: (i % ntab, 0)
    tok_spec = pl.BlockSpec((None, tm, GROUP_W), row)
    tok = lambda dt: jax.ShapeDtypeStruct((B, T, GROUP_W), dt)
    il_spec = pl.BlockSpec((None, tm * n_a_heads, A_VDIM), row)
    il = jax.ShapeDtypeStruct((B, T * n_a_heads, A_VDIM), F32)
    tr_spec = pl.BlockSpec((None, GROUP_W, tm), colm)
    tr = lambda dt: jax.ShapeDtypeStruct((B, GROUP_W, T), dt)
    lft_spec = pl.BlockSpec((None, SUBLANES, tm), colm)
    lft = jax.ShapeDtypeStruct((B, SUBLANES, T), F32)
    if prompt:
        out_specs = [tok_spec, il_spec, il_spec, tok_spec, tok_spec, tok_spec, tr_spec, tr_spec, tr_spec, tr_spec,
                     lft_spec]
        out_shape = [tok(BF16), il, il, tok(BF16), tok(BF16), tok(BF16), tr(F32), tr(F32), tr(BF16), tr(BF16), lft]
    else:
        out_specs = [tok_spec, il_spec, il_spec, tok_spec, tok_spec, tok_spec, lft_spec]
        out_shape = [tok(BF16), il, il, tok(BF16), tok(F32), tok(F32), lft]
    return pl.pallas_call(
        functools.partial(_proj_kernel, prompt=prompt),
        grid=(B, nt),
        in_specs=[
            pl.BlockSpec((None, tm, D), row),
            pl.BlockSpec((1, D), const),
            pl.BlockSpec(w_bf.shape, const),
            pl.BlockSpec((1, LANES), const),
            pl.BlockSpec((tm, LANES), tab),
            pl.BlockSpec((tm, LANES), tab),
        ],
        out_specs=out_specs,
        out_shape=out_shape,
        compiler_params=pltpu.CompilerParams(
            dimension_semantics=("arbitrary", "arbitrary"), vmem_limit_bytes=VMEM_LIMIT),
        name="proj",
    )(x3, g, w_bf, bfp, cos_t, sin_t)


_NT = (((1,), (1,)), ((), ()))


def _flash_groups(groups, m_sc, acc_sc, *, seq, tq):
    nq = seq // tq
    half = tq // 2
    lane = lax.broadcasted_iota(jnp.int32, (tq, LANES), 1)

    def qblock(i):
        q0 = i * tq
        qs = []
        for q_ref, *_ in groups:
            q = q_ref[pl.ds(q0, tq), :]
            zero = jnp.zeros_like(q)
            qs.append((jnp.where(lane < HEAD_DIM, q, zero), jnp.where(lane >= HEAD_DIM, q, zero)))
        m_sc[...] = jnp.full(m_sc.shape, NEG_INF, F32)
        acc_sc[...] = jnp.zeros(acc_sc.shape, F32)

        def block(r0, nr, k0, kw, masked):
            if masked:
                rowg = q0 + r0 + lax.broadcasted_iota(jnp.int32, (nr, kw), 0)
                colg = k0 + lax.broadcasted_iota(jnp.int32, (nr, kw), 1)
                keep = colg <= rowg
            for g, (_, k_ref, vext_sc, bias_row, _, k_transposed) in enumerate(groups):
                if k_transposed:
                    k = k_ref[:, pl.ds(k0, kw)]
                    v = vext_sc[:, pl.ds(k0, kw)]
                else:
                    k = k_ref[pl.ds(k0, kw), :]
                    v = vext_sc[pl.ds(k0, kw), :]
                for c in range(2):
                    n = 2 * g + c
                    qc = qs[g][c][r0:r0 + nr]
                    if k_transposed:
                        s = jnp.dot(qc, k, preferred_element_type=F32)
                    else:
                        s = lax.dot_general(qc, k, _NT, preferred_element_type=F32)
                    if bias_row is not None:
                        s = s - bias_row(c, k0, kw)
                    if masked:
                        s = jnp.where(keep, s, NEG_INF)
                    m_prev = m_sc[n, r0:r0 + nr, :]
                    m_new = jnp.maximum(m_prev, jnp.max(s, axis=1, keepdims=True))
                    corr = jnp.exp2(m_prev - m_new)
                    p = jnp.exp2(s - jnp.concatenate([m_new] * (kw // LANES), axis=1))
                    pb = p.astype(BF16)
                    if k_transposed:
                        pv = lax.dot_general(pb, v, _NT, preferred_element_type=F32)
                    else:
                        pv = jnp.dot(pb, v, preferred_element_type=F32)
                    acc_sc[n, r0:r0 + nr, :] = acc_sc[n, r0:r0 + nr, :] * jnp.concatenate([corr, corr], axis=1) + pv
                    m_sc[n, r0:r0 + nr, :] = m_new

        for j in range(i):
            block(0, tq, j * tq, tq, False)
        block(0, tq, q0, half, True)
        block(half, half, q0 + half, half, True)
        for g, grp in enumerate(groups):
            acc0 = acc_sc[2 * g]
            acc1 = acc_sc[2 * g + 1]
            grp[4](q0, acc0[:, :LANES] / acc0[:, LANES:], acc1[:, :LANES] / acc1[:, LANES:])

    for i in range(nq):
        qblock(i)


def _prompt_attn_kernel(qa_ref, ka_ref, va_ref, qb_ref, kbt_ref, vbt_ref, lft_ref, lamv_ref, gs_ref,
                        oa_ref, ob_ref,
                        vaext_sc, vbext_sc, cum_sc, m_sc, acc_sc, *, tq, lam_init):
    seq = qa_ref.shape[0]
    j = pl.program_id(1)

    @pl.when(j == 0)
    def _():
        carry = jnp.zeros((SUBLANES, LANES), F32)
        for c in range(seq // LANES):
            x = lft_ref[:, c * LANES:(c + 1) * LANES]
            cum_sc[:, c * LANES:(c + 1) * LANES] = (_cumsum_lanes(x) + carry) * LOG2E
            carry = carry + jnp.sum(x, axis=1, keepdims=True)

    vaext_sc[:, :LANES] = va_ref[...]
    vaext_sc[:, LANES:] = jnp.ones((seq, LANES), BF16)
    vbext_sc[:LANES, :] = vbt_ref[...]
    vbext_sc[LANES:, :] = jnp.ones((LANES, seq), BF16)

    lam = _lam_value(lamv_ref, lam_init)
    gs = gs_ref[...]
    lane = lax.broadcasted_iota(jnp.int32, (tq, LANES), 1)

    def write_a(q0, o1, o2):
        o = o1 - lam * o2
        oa_ref[pl.ds(q0, tq), :] = (_rms(o, gs) * (1.0 - lam_init)).astype(BF16)

    def write_b(q0, o1, o2):
        ob_ref[pl.ds(q0, tq), :] = jnp.where(lane < HEAD_DIM, o1, o2).astype(BF16)

    def bias_b(c, k0, kw):
        return cum_sc[pl.ds(2 * j + c, 1), pl.ds(k0, kw)]

    _flash_groups([(qa_ref, ka_ref, vaext_sc, None, write_a, False),
                   (qb_ref, kbt_ref, vbext_sc, bias_b, write_b, True)], m_sc, acc_sc, seq=seq, tq=tq)


def _prompt_attn_call(qa, ka, va, qb, kbt, vbt, lft, lamv, gs, *, lam_init, tq):
    B, T, _ = qa.shape
    n_pair = GROUP_W // LANES
    tokc = pl.BlockSpec((None, T, LANES), lambda b, j: (b, 0, j))
    tr = pl.BlockSpec((None, LANES, T), lambda b, j: (b, j, 0))
    const2 = lambda b, j: (0, 0)
    return pl.pallas_call(
        functools.partial(_prompt_attn_kernel, tq=tq, lam_init=lam_init),
        grid=(B, n_pair),
        in_specs=[tokc, tokc, tokc, tokc, tr, tr,
                  pl.BlockSpec((None, SUBLANES, T), lambda b, j: (b, 0, 0)),
                  pl.BlockSpec((4, HEAD_DIM), const2),
                  pl.BlockSpec((1, A_VDIM), const2)],
        out_specs=[tokc, tokc],
        out_shape=[jax.ShapeDtypeStruct((B, T, GROUP_W), BF16), jax.ShapeDtypeStruct((B, T, GROUP_W), BF16)],
        scratch_shapes=[
            pltpu.VMEM((T, 2 * LANES), BF16), pltpu.VMEM((2 * LANES, T), BF16),
            pltpu.VMEM((SUBLANES, T), F32),
            pltpu.VMEM((4, tq, LANES), F32), pltpu.VMEM((4, tq, 2 * LANES), F32),
        ],
        compiler_params=pltpu.CompilerParams(
            dimension_semantics=("arbitrary", "arbitrary"), vmem_limit_bytes=VMEM_LIMIT),
        name="prompt_attn",
    )(qa, ka, va, qb, kbt, vbt, lft, lamv, gs)


def _sample_attn_kernel(pt_ref, qbda_ref, qbdb_ref, kan_ref, van_ref, kbn_ref, vbn_ref, lfn_ref, lamv_ref, gs_ref,
                        *rest, n_pp, n_a_heads, lam_init):
    pages = rest[:5 * n_pp]
    oa_ref, ob_ref = rest[5 * n_pp:5 * n_pp + 2]
    m_sc, l_sc, acca_sc, accbt_sc, carry_sc = rest[5 * n_pp + 2:]
    xk = pages[0 * n_pp:1 * n_pp]
    xv = pages[1 * n_pp:2 * n_pp]
    kt = pages[2 * n_pp:3 * n_pp]
    vt = pages[3 * n_pp:4 * n_pp]
    lf = pages[4 * n_pp:5 * n_pp]
    g = pl.program_id(1)
    n_rows = 2 * SUBLANES * n_a_heads
    tn = SUBLANES

    @pl.when(g == 0)
    def _():
        m_sc[...] = jnp.full(m_sc.shape, NEG_INF, F32)
        l_sc[...] = jnp.zeros(l_sc.shape, F32)
        acca_sc[...] = jnp.zeros(acca_sc.shape, F32)
        accbt_sc[...] = jnp.zeros(accbt_sc.shape, F32)
        carry_sc[...] = jnp.zeros(carry_sc.shape, F32)

    qbda = qbda_ref[...]
    qbdb = qbdb_ref[...]
    eye = (lax.broadcasted_iota(jnp.int32, (LANES, LANES), 0)
           == lax.broadcasted_iota(jnp.int32, (LANES, LANES), 1)).astype(F32)

    def attend(k_heads, v_heads, ktile, vtile, ck, keep):
        w = ck.shape[1]
        s_rows = [lax.dot_general(qbda[h * 2 * tn:(h + 1) * 2 * tn], k_heads[h], _NT, preferred_element_type=F32)
                  for h in range(n_a_heads)]
        bias = jnp.broadcast_to(ck[:, None, :], (SUBLANES, tn, w)).reshape(SUBLANES * tn, w)
        s_rows.append(jnp.dot(qbdb, ktile, preferred_element_type=F32) - bias)
        s = jnp.concatenate(s_rows, axis=0)
        if keep is not None:
            s = jnp.where(keep, s, NEG_INF)
        m_prev = m_sc[...]
        m_new = jnp.maximum(m_prev, jnp.max(s, axis=1, keepdims=True))
        corr = jnp.exp(m_prev - m_new)
        p = jnp.exp(s - jnp.concatenate([m_new] * (w // LANES), axis=1))
        l_sc[...] = corr * l_sc[...] + jnp.sum(p, axis=1, keepdims=True)
        m_sc[...] = m_new
        pb = p.astype(BF16)
        for h in range(n_a_heads):
            r = slice(h * 2 * tn, (h + 1) * 2 * tn)
            acca_sc[r, :] = acca_sc[r, :] * corr[r] + jnp.dot(pb[r], v_heads[h], preferred_element_type=F32)
        corr_row = jnp.sum(eye * corr, axis=0, keepdims=True)
        accbt_sc[...] = accbt_sc[...] * corr_row + lax.dot_general(vtile, pb, _NT, preferred_element_type=F32)

    carry = carry_sc[...]
    cks = []
    for j in range(n_pp):
        x = lf[j][...]
        cks.append(_cumsum_lanes(x) + carry)
        carry = carry + jnp.sum(x, axis=1, keepdims=True)
    page = xk[0].shape[0] // n_a_heads

    def head_rows(refs, h):
        return jnp.concatenate([r[pl.ds(h, page, stride=n_a_heads), :] for r in refs], axis=0).astype(BF16)

    attend([head_rows(xk, h) for h in range(n_a_heads)],
           [head_rows(xv, h) for h in range(n_a_heads)],
           jnp.concatenate([r[...] for r in kt], axis=1).astype(BF16),
           jnp.concatenate([r[...] for r in vt], axis=1).astype(BF16),
           jnp.concatenate(cks, axis=1), None)
    carry_sc[...] = carry

    @pl.when(g == pl.num_programs(1) - 1)
    def _():
        pad = lambda a: jnp.concatenate([a, jnp.zeros((LANES - tn, a.shape[1]), a.dtype)], axis=0)
        new_head = lambda ref, h: pad(ref[pl.ds(h, tn, stride=n_a_heads), :]).astype(BF16)
        ck_new = _cumsum_lanes(lfn_ref[...]) + carry
        rowq = lax.broadcasted_iota(jnp.int32, (LANES, LANES), 0) % tn
        key = lax.broadcasted_iota(jnp.int32, (LANES, LANES), 1)
        attend([new_head(kan_ref, h) for h in range(n_a_heads)],
               [new_head(van_ref, h) for h in range(n_a_heads)],
               pad(kbn_ref[...]).T.astype(BF16), pad(vbn_ref[...]).T.astype(BF16),
               ck_new, key <= rowq)

        lam = _lam_value(lamv_ref, lam_init)
        gs = gs_ref[...]
        inv_l = 1.0 / l_sc[...]
        for h in range(n_a_heads):
            r1 = slice(h * 2 * tn, h * 2 * tn + tn)
            r2 = slice(h * 2 * tn + tn, (h + 1) * 2 * tn)
            o = acca_sc[r1, :] * inv_l[r1] - lam * (acca_sc[r2, :] * inv_l[r2])
            oa_ref[:, h * A_VDIM:(h + 1) * A_VDIM] = (_rms(o, gs) * (1.0 - lam_init)).astype(BF16)
        inv_lb = jnp.concatenate([inv_l[n_rows:]] * (GROUP_W // LANES), axis=1)
        ob_full = accbt_sc[...].T[n_rows:, :] * inv_lb
        lane = lax.broadcasted_iota(jnp.int32, (tn, LANES), 1)
        for jb in range(GROUP_W // LANES):
            lo = ob_full[(2 * jb) * tn:(2 * jb + 1) * tn, jb * LANES:(jb + 1) * LANES]
            hi = ob_full[(2 * jb + 1) * tn:(2 * jb + 2) * tn, jb * LANES:(jb + 1) * LANES]
            ob_ref[:, jb * LANES:(jb + 1) * LANES] = jnp.where(lane < HEAD_DIM, lo, hi).astype(BF16)


def _sample_attn_call(pt_flat, qbda, qbdb, kan, van, kbn, vbn, lfn, lamv, gs, xk, xv, kt, vt, lf,
                      *, n_pages, n_pp, lam_init):
    Bd = qbda.shape[0]
    n_a_heads = GROUP_W // A_VDIM
    steps = n_pages // n_pp
    per_b = lambda b, g, pt: (b, 0, 0)
    tok = pl.BlockSpec((SUBLANES, GROUP_W), lambda b, g, pt: (b, 0))
    tok_il = pl.BlockSpec((SUBLANES * n_a_heads, A_VDIM), lambda b, g, pt: (b, 0))
    const2 = lambda b, g, pt: (0, 0)

    def page_spec(rows, j):
        return pl.BlockSpec((None, rows, LANES), lambda b, g, pt: (pt[b * n_pages + g * n_pp + j], 0, 0))

    page_specs, page_args = [], []
    for arr in (xk, xv, kt, vt, lf):
        for j in range(n_pp):
            page_specs.append(page_spec(arr.shape[1], j))
            page_args.append(arr)
    grid_spec = pltpu.PrefetchScalarGridSpec(
        num_scalar_prefetch=1,
        grid=(Bd, steps),
        in_specs=[pl.BlockSpec((None, 2 * SUBLANES * n_a_heads, LANES), per_b),
                  pl.BlockSpec((None, SUBLANES * SUBLANES, GROUP_W), per_b),
                  tok_il, tok_il, tok, tok,
                  pl.BlockSpec((None, SUBLANES, LANES), per_b),
                  pl.BlockSpec((4, HEAD_DIM), const2),
                  pl.BlockSpec((1, A_VDIM), const2)] + page_specs,
        out_specs=[tok, tok],
        scratch_shapes=[pltpu.VMEM((LANES, LANES), F32), pltpu.VMEM((LANES, LANES), F32),
                        pltpu.VMEM((2 * SUBLANES * n_a_heads, LANES), F32),
                        pltpu.VMEM((GROUP_W, LANES), F32),
                        pltpu.VMEM((SUBLANES, LANES), F32)],
    )
    n_tok = Bd * SUBLANES
    return pl.pallas_call(
        functools.partial(_sample_attn_kernel, n_pp=n_pp, n_a_heads=n_a_heads, lam_init=lam_init),
        grid_spec=grid_spec,
        out_shape=[jax.ShapeDtypeStruct((n_tok, GROUP_W), BF16), jax.ShapeDtypeStruct((n_tok, GROUP_W), BF16)],
        compiler_params=pltpu.CompilerParams(
            dimension_semantics=("arbitrary", "arbitrary"), vmem_limit_bytes=VMEM_LIMIT),
        name="sample_attn",
    )(pt_flat, qbda, qbdb, kan, van, kbn, vbn, lfn, lamv, gs, *page_args)


def _post_kernel(oa_ref, ob_ref, x_ref, wo_ref, g1_ref, g2_ref, wg_ref, wu_ref, wd_ref, g3_ref, y_ref, *, ff_chunk):
    mix = jnp.concatenate([oa_ref[...], ob_ref[...]], axis=1)
    a = jnp.dot(mix, wo_ref[...], preferred_element_type=F32)
    x1 = x_ref[...] + _rms(a, g1_ref[...])
    hb = _rms(x1, g2_ref[...]).astype(BF16)
    d_ff = wg_ref.shape[1]
    d = None
    bounds = list(range(0, d_ff, ff_chunk)) + [d_ff]
    for c0, c1 in zip(bounds[:-1], bounds[1:]):
        cs = slice(c0, c1)
        gate = jnp.dot(hb, wg_ref[:, cs], preferred_element_type=F32)
        up = jnp.dot(hb, wu_ref[:, cs], preferred_element_type=F32)
        f = (gate * jax.nn.sigmoid(gate) * up).astype(BF16)
        part = jnp.dot(f, wd_ref[cs, :], preferred_element_type=F32)
        d = part if d is None else d + part
    y_ref[...] = x1 + _rms(d, g3_ref[...])


def _post_call(oa, ob, x2, wo, g1, g2, wg, wu, wd, g3, *, tm, ff_chunk):
    N, D = x2.shape
    row = lambda i: (i, 0)
    const = lambda i: (0, 0)
    resident = lambda a: pl.BlockSpec(a.shape, const, pipeline_mode=pl.Buffered(1))
    vec = pl.BlockSpec((1, D), const)
    return pl.pallas_call(
        functools.partial(_post_kernel, ff_chunk=ff_chunk),
        grid=(N // tm,),
        in_specs=[pl.BlockSpec((tm, GROUP_W), row), pl.BlockSpec((tm, GROUP_W), row), pl.BlockSpec((tm, D), row),
                  resident(wo), vec, vec, resident(wg), resident(wu), resident(wd), vec],
        out_specs=pl.BlockSpec((tm, D), row),
        out_shape=jax.ShapeDtypeStruct((N, D), F32),
        compiler_params=pltpu.CompilerParams(
            dimension_semantics=("arbitrary",), vmem_limit_bytes=VMEM_LIMIT),
        name="post",
    )(oa, ob, x2, wo, g1, g2, wg, wu, wd, g3)


def _rope_tables(pos):
    half = HEAD_DIM // 2
    inv = ROPE_THETA ** (-jnp.arange(half, dtype=F32) / half)
    ang = pos[:, None] * inv[None, :]
    cos = jnp.tile(jnp.cos(ang), (1, LANES // half))
    sign = jnp.where((jnp.arange(LANES) % HEAD_DIM) < half, -1.0, 1.0).astype(F32)
    sin = jnp.tile(jnp.sin(ang), (1, LANES // half)) * sign[None, :]
    return cos, sin


def _pick(n, prefs):
    for p in prefs:
        if n % p == 0:
            return p
    return n


def kernel(x_prompt, x_sample, cache_a_k, cache_a_v, cache_b_k, cache_b_v, cache_b_logf, page_table,
           g_pre_mix, w_in, b_f, lam_q1, lam_k1, lam_q2, lam_k2, g_subln, w_out, g_post_mix,
           g_pre_ffn, w_gate, w_up, w_down, g_post_ffn):
    B, T, D = x_prompt.shape
    Bd, Tn, _ = x_sample.shape
    depth = w_in.shape[0]
    n_pool, page = cache_a_k.shape[1], cache_a_k.shape[2]
    n_pages = page_table.shape[1]
    past_len = n_pages * page
    d_ff = w_gate.shape[2]
    n_b_heads = cache_b_k.shape[3]
    in_cols = w_in.shape[2]
    assert D == 2 * GROUP_W and Tn == SUBLANES and page == LANES and n_b_heads == SUBLANES
    assert in_cols == 6 * GROUP_W + n_b_heads

    tm_p = _pick(T, (512, 256, 128))
    n_tok_s = Bd * Tn
    tm_s = _pick(n_tok_s, (512, 256, 128, 8))
    tq = _pick(T, (512, 256, 128))
    n_pp = _pick(n_pages, (8, 4, 2, 1))
    ff_chunk = MXU_WIDTH * -(-d_ff // (2 * MXU_WIDTH))

    cos_p, sin_p = _rope_tables(jnp.arange(T, dtype=F32))
    pos_s = past_len + jnp.arange(Tn, dtype=F32)
    cos_s, sin_s = _rope_tables(jnp.tile(pos_s, tm_s // Tn))

    pt_flat = page_table.reshape(-1).astype(jnp.int32)
    yp = x_prompt
    ys = x_sample.reshape(1, n_tok_s, D)
    rows_p, rows_s = [], []
    for layer in range(depth):
        lam_init = _lambda_init(layer)
        w_bf = jnp.pad(w_in[layer].astype(BF16), ((0, 0), (0, 6 * GROUP_W + LANES - in_cols)))
        bfp = jnp.pad(b_f[layer], (0, LANES - n_b_heads)).reshape(1, LANES)
        g0 = g_pre_mix[layer].reshape(1, D)
        lamv = jnp.stack([lam_q1[layer], lam_k1[layer], lam_q2[layer], lam_k2[layer]]).astype(F32)
        gs = g_subln[layer].reshape(1, A_VDIM)
        post_w = (w_out[layer].astype(BF16), g_post_mix[layer].reshape(1, D), g_pre_ffn[layer].reshape(1, D),
                  w_gate[layer].astype(BF16), w_up[layer].astype(BF16), w_down[layer].astype(BF16),
                  g_post_ffn[layer].reshape(1, D))

        qa, ka, va, ka_bf, va_bf, qb, kbt, vbt, kbt_bf, vbt_bf, lft = _proj_call(
            yp, g0, w_bf, bfp, cos_p, sin_p, tm=tm_p, prompt=True)
        oa, ob = _prompt_attn_call(qa, ka_bf, va_bf, qb, kbt_bf, vbt_bf, lft, lamv, gs, lam_init=lam_init, tq=tq)
        yp = _post_call(oa.reshape(B * T, GROUP_W), ob.reshape(B * T, GROUP_W), yp.reshape(B * T, D), *post_w,
                        tm=tm_p, ff_chunk=ff_chunk).reshape(B, T, D)
        n_a_heads = GROUP_W // A_VDIM
        rows_p.append((ka.reshape(B, T, n_a_heads, A_VDIM), va.reshape(B, T, n_a_heads, A_VDIM),
                       kbt.reshape(B, n_b_heads, HEAD_DIM, T).transpose(0, 3, 1, 2),
                       vbt.reshape(B, n_b_heads, HEAD_DIM, T).transpose(0, 3, 1, 2),
                       lft.transpose(0, 2, 1)))

        qa_s, ka_s, va_s, qb_s, kb_s, vb_s, lft_s = _proj_call(ys, g0, w_bf, bfp, cos_s, sin_s, tm=tm_s,
                                                               prompt=False)
        qa5 = qa_s.reshape(Bd, Tn, n_a_heads, 2, HEAD_DIM).transpose(0, 2, 3, 1, 4)
        qbda = (qa5[:, :, :, :, None, :] * jnp.eye(2, dtype=BF16)[None, None, :, None, :, None]
                ).reshape(Bd, n_a_heads * 2 * Tn, A_VDIM)
        qb4 = qb_s.reshape(Bd, Tn, n_b_heads, HEAD_DIM).transpose(0, 2, 1, 3)
        qbdb = (qb4[:, :, :, None, :] * jnp.eye(n_b_heads, dtype=BF16)[None, :, None, :, None]
                ).reshape(Bd, n_b_heads * Tn, GROUP_W)
        lfn = jnp.pad(lft_s[0].reshape(n_b_heads, Bd, Tn).transpose(1, 0, 2), ((0, 0), (0, 0), (0, LANES - Tn)))
        xk = cache_a_k[layer].reshape(n_pool, page * n_a_heads, A_VDIM)
        xv = cache_a_v[layer].reshape(n_pool, page * n_a_heads, A_VDIM)
        kt = cache_b_k[layer].transpose(0, 2, 3, 1).reshape(n_pool, n_b_heads * HEAD_DIM, page)
        vt = cache_b_v[layer].transpose(0, 2, 3, 1).reshape(n_pool, n_b_heads * HEAD_DIM, page)
        lf = cache_b_logf[layer].transpose(0, 2, 1)
        oa_s, ob_s = _sample_attn_call(pt_flat, qbda, qbdb, ka_s[0], va_s[0], kb_s[0], vb_s[0], lfn, lamv, gs,
                                       xk, xv, kt, vt, lf, n_pages=n_pages, n_pp=n_pp, lam_init=lam_init)
        ys = _post_call(oa_s, ob_s, ys[0], *post_w, tm=tm_s, ff_chunk=ff_chunk).reshape(1, n_tok_s, D)
        rows_s.append((ka_s.reshape(Bd, Tn, n_a_heads, A_VDIM), va_s.reshape(Bd, Tn, n_a_heads, A_VDIM),
                       kb_s.reshape(Bd, Tn, n_b_heads, HEAD_DIM), vb_s.reshape(Bd, Tn, n_b_heads, HEAD_DIM),
                       lft_s[0].reshape(n_b_heads, Bd, Tn).transpose(1, 2, 0)))

    stack = lambda rows, i: jnp.stack([r[i] for r in rows], axis=0)
    return (yp, ys.reshape(Bd, Tn, D),
            stack(rows_p, 0), stack(rows_p, 1), stack(rows_p, 2), stack(rows_p, 3), stack(rows_p, 4),
            stack(rows_s, 0), stack(rows_s, 1), stack(rows_s, 2), stack(rows_s, 3), stack(rows_s, 4))
```
